```python
import math, functools
import jax, jax.numpy as jnp
from jax import lax
import numpy as np

D_MODEL = 1024
BATCH = 2
SEQ = 8192
DEPTH = 4
DEC_BATCH = 128
DEC_SEQ = 8
PAST_LEN = 2048
PAGE_SIZE = 128

D_SSM = D_MODEL // 2
SSM_GROUP = 16
N_SSM_GROUPS = D_SSM // SSM_GROUP
SSM_STATE = 64
D_CONV = D_MODEL // 2
CONV_WIDTH = 31
N_HEADS = 8
HEAD_DIM = 64
D_ATTN = N_HEADS * HEAD_DIM
N_IDX_HEADS = 8
IDX_DIM = 64
TOPK_MAX = 256
Q_BLOCK = 128
ATTN_SCALE = HEAD_DIM ** -0.5
N_BRANCHES = 3
D_IN = D_SSM + 2 * D_CONV + 3 * D_ATTN + N_IDX_HEADS * IDX_DIM + IDX_DIM + N_IDX_HEADS + N_BRANCHES * D_MODEL
D_FF = 2816
N_EXPERTS = 8
TOP_K = 2
N_DENSE = (DEPTH + 1) // 2
N_MOE = DEPTH // 2
DN_ALPHA = (2.0 * DEPTH) ** 0.25
DN_BETA = (8.0 * DEPTH) ** -0.25
LN_EPS = 1e-5
F32 = jnp.float32
SPLIT_POINTS = [int(c) for c in np.cumsum([D_SSM, 2 * D_CONV, D_ATTN, D_ATTN, D_ATTN, N_IDX_HEADS * IDX_DIM, IDX_DIM, N_IDX_HEADS])]

kernel_name = 'hybrid_s5_conformer_dsa_deepnorm_step'


def _layer_norm(x, g, b):
    xf = x.astype(F32)
    mu = jnp.mean(xf, axis=-1, keepdims=True)
    var = jnp.mean(jnp.square(xf - mu), axis=-1, keepdims=True)
    return ((xf - mu) * lax.rsqrt(var + LN_EPS) * g.astype(F32) + b.astype(F32)).astype(x.dtype)


def _swiglu(x, w_gate, w_up, w_down):
    return (jax.nn.silu(x @ w_gate) * (x @ w_up)) @ w_down


def _moe(x, w_router, w_gate, w_up, w_down):
    logits = jnp.einsum('ntd,de->nte', x, w_router, preferred_element_type=F32)
    top_val, top_idx = lax.top_k(logits, TOP_K)
    top_p = jax.nn.softmax(top_val, axis=-1)
    gates = jnp.einsum('ntk,ntke->nte', top_p, jax.nn.one_hot(top_idx, N_EXPERTS, dtype=F32)).astype(x.dtype)
    out = jnp.zeros_like(x)
    for e in range(N_EXPERTS):
        out = out + gates[..., e:e + 1] * _swiglu(x, w_gate[e], w_up[e], w_down[e])
    return out


def _complex_affine_combine(earlier, later):
    a1r, a1i, b1r, b1i = earlier
    a2r, a2i, b2r, b2i = later
    return (a1r * a2r - a1i * a2i, a1r * a2i + a1i * a2r,
            a2r * b1r - a2i * b1i + b2r, a2r * b1i + a2i * b1r + b2i)


def _s5_branch(u, h0_re, h0_im, a_re, a_im, log_dt, b_re, b_im, c_re, c_im, d_skip, w_glu):
    n, t, _ = u.shape
    uf = u.astype(F32).reshape(n, t, N_SSM_GROUPS, SSM_GROUP)
    a_re = a_re.astype(F32)
    a_im = a_im.astype(F32)
    dt = jnp.exp(log_dt.astype(F32))[:, None]
    mag = jnp.exp(a_re * dt)
    ab_re = mag * jnp.cos(a_im * dt)
    ab_im = mag * jnp.sin(a_im * dt)
    den = a_re * a_re + a_im * a_im
    nr = ab_re - 1.0
    f_re = (nr * a_re + ab_im * a_im) / den
    f_im = (ab_im * a_re - nr * a_im) / den
    bu_re = jnp.einsum('gpc,ntgc->ntgp', b_re.astype(F32), uf)
    bu_im = jnp.einsum('gpc,ntgc->ntgp', b_im.astype(F32), uf)
    x_re = f_re * bu_re - f_im * bu_im
    x_im = f_re * bu_im + f_im * bu_re
    h0_re = h0_re.astype(F32)
    h0_im = h0_im.astype(F32)
    x_re = x_re.at[:, 0].add(ab_re * h0_re - ab_im * h0_im)
    x_im = x_im.at[:, 0].add(ab_re * h0_im + ab_im * h0_re)
    aa_re = jnp.broadcast_to(ab_re, x_re.shape)
    aa_im = jnp.broadcast_to(ab_im, x_re.shape)
    _, _, h_re, h_im = lax.associative_scan(_complex_affine_combine, (aa_re, aa_im, x_re, x_im), axis=1)
    y = jnp.einsum('gcp,ntgp->ntgc', c_re.astype(F32), h_re) - jnp.einsum('gcp,ntgp->ntgc', c_im.astype(F32), h_im)
    y = (y + d_skip.astype(F32).reshape(N_SSM_GROUPS, SSM_GROUP) * uf).reshape(n, t, D_SSM)
    y = jax.nn.gelu(y).astype(u.dtype)
    a, g = jnp.split(y @ w_glu, 2, axis=-1)
    return a * jax.nn.sigmoid(g), h_re[:, -1], h_im[:, -1]


def _conv_branch(zc, conv_buf, w_dw, b_dw, ln_g, ln_b, w_pw):
    a, g = jnp.split(zc, 2, axis=-1)
    v = a * jax.nn.sigmoid(g)
    vp = jnp.concatenate([conv_buf.astype(v.dtype), v], axis=1)
    y = lax.conv_general_dilated(vp, w_dw[:, None, :].astype(v.dtype), window_strides=(1,), padding='VALID',
                                 dimension_numbers=('NWC', 'WIO', 'NWC'), feature_group_count=D_CONV)
    y = jax.nn.silu(_layer_norm(y + b_dw, ln_g, ln_b))
    return y @ w_pw, vp[:, -(CONV_WIDTH - 1):]


def _index_scores(qi, ki, wi):
    dots = jnp.einsum('nqhd,nsd->nqhs', qi, ki, preferred_element_type=F32)
    return jnp.einsum('nqh,nqhs->nqs', wi.astype(F32), jax.nn.relu(dots))


def _gather_rows(rows, idx):
    return jax.vmap(lambda r, i: r[i])(rows, idx)


def _sparse_attend(q, ks, vs, valid):
    s = jnp.einsum('nqhd,nqkhd->nqhk', q, ks, preferred_element_type=F32) * ATTN_SCALE
    s = jnp.where(valid[:, :, None, :], s, -jnp.inf)
    p = jax.nn.softmax(s, axis=-1)
    return jnp.einsum('nqhk,nqkhd->nqhd', p.astype(vs.dtype), vs, preferred_element_type=F32).astype(q.dtype)


def _dsa_prompt(q, k, v, qi, ki, wi):
    n, t = q.shape[:2]
    k_sel = min(TOPK_MAX, t // 4)
    key_pos = jnp.arange(t)

    def block(i):
        start = i * Q_BLOCK
        qb = lax.dynamic_slice_in_dim(q, start, Q_BLOCK, axis=1)
        qib = lax.dynamic_slice_in_dim(qi, start, Q_BLOCK, axis=1)
        wib = lax.dynamic_slice_in_dim(wi, start, Q_BLOCK, axis=1)
        qpos = start + jnp.arange(Q_BLOCK)
        score = _index_scores(qib, ki, wib)
        score = jnp.where(key_pos[None, None, :] <= qpos[None, :, None], score, -jnp.inf)
        _, idx = lax.top_k(score, k_sel)
        valid = idx <= qpos[None, :, None]
        return _sparse_attend(qb, _gather_rows(k, idx), _gather_rows(v, idx), valid)

    out = lax.map(block, jnp.arange(t // Q_BLOCK))
    return jnp.moveaxis(out, 0, 1).reshape(n, t, D_ATTN)


def _dsa_sample(q, k, v, qi, ki, wi, pool_k, pool_v, pool_ki, page_table):
    n, t = q.shape[:2]
    past = page_table.shape[1] * PAGE_SIZE
    total = past + t
    k_sel = min(TOPK_MAX, total // 4)
    ki_past = pool_ki[page_table].reshape(n, past, IDX_DIM)
    ki_all = jnp.concatenate([ki_past.astype(ki.dtype), ki], axis=1)
    qpos = past + jnp.arange(t)
    key_pos = jnp.arange(total)
    score = _index_scores(qi, ki_all, wi)
    score = jnp.where(key_pos[None, None, :] <= qpos[None, :, None], score, -jnp.inf)
    _, idx = lax.top_k(score, k_sel)
    valid = idx <= qpos[None, :, None]
    from_past = idx < past
    pidx = jnp.minimum(idx, past - 1)
    phys_page = jax.vmap(lambda pt, lp: pt[lp])(page_table, pidx // PAGE_SIZE)
    phys = phys_page * PAGE_SIZE + pidx % PAGE_SIZE
    ks_past = pool_k.reshape((-1, N_HEADS, HEAD_DIM))[phys]
    vs_past = pool_v.reshape((-1, N_HEADS, HEAD_DIM))[phys]
    sel_new = jax.nn.one_hot(idx - past, t, dtype=F32)
    s_past = jnp.einsum('nqhd,nqkhd->nqhk', q, ks_past, preferred_element_type=F32)
    s_new_all = jnp.einsum('nqhd,njhd->nqhj', q, k, preferred_element_type=F32)
    s_new = jnp.einsum('nqhj,nqkj->nqhk', s_new_all, sel_new)
    s = jnp.where(from_past[:, :, None, :], s_past, s_new) * ATTN_SCALE
    s = jnp.where(valid[:, :, None, :], s, -jnp.inf)
    p = jax.nn.softmax(s, axis=-1)
    p_past = jnp.where(from_past[:, :, None, :], p, 0.0)
    p_new = jnp.einsum('nqhk,nqkj->nqhj', p, sel_new)
    out = (jnp.einsum('nqhk,nqkhd->nqhd', p_past.astype(vs_past.dtype), vs_past, preferred_element_type=F32)
           + jnp.einsum('nqhj,njhd->nqhd', p_new, v.astype(F32)))
    return out.astype(q.dtype).reshape(n, t, D_ATTN)


def _mixer(x, h0_re, h0_im, conv_buf, attend, w_in, a_re, a_im, log_dt, b_re, b_im, c_re, c_im, d_skip,
           w_glu, w_dw, b_dw, cln_g, cln_b, w_pw, w_o, w_out):
    n, t, _ = x.shape
    u, zc, q, k, v, qi, ki, wi, gl = jnp.split(x @ w_in, SPLIT_POINTS, axis=-1)
    q = q.reshape(n, t, N_HEADS, HEAD_DIM)
    k = k.reshape(n, t, N_HEADS, HEAD_DIM)
    v = v.reshape(n, t, N_HEADS, HEAD_DIM)
    qi = qi.reshape(n, t, N_IDX_HEADS, IDX_DIM)
    y_a, h_re, h_im = _s5_branch(u, h0_re, h0_im, a_re, a_im, log_dt, b_re, b_im, c_re, c_im, d_skip, w_glu)
    y_b, new_buf = _conv_branch(zc, conv_buf, w_dw, b_dw, cln_g, cln_b, w_pw)
    y_c = attend(q, k, v, qi, ki, wi) @ w_o
    g = jax.nn.sigmoid(gl.reshape(n, t, N_BRANCHES, D_MODEL))
    merged = g[:, :, 0] * y_a + g[:, :, 1] * y_b + g[:, :, 2] * y_c
    return merged @ w_out, h_re, h_im, new_buf, k, v, ki


def setup_inputs(seed: int = 0) -> dict:
    key = jax.random.key(seed)
    ks = jax.random.split(key, 40)

    def nrm(k, shape, scale):
        return scale * jax.random.normal(k, shape, F32)

    n_pages = PAST_LEN // PAGE_SIZE
    n_used = DEC_BATCH * n_pages
    n_pool = n_used + n_used // 4
    page_table = jax.random.permutation(ks[8], n_pool)[:n_used].reshape(DEC_BATCH, n_pages).astype(jnp.int32)
    G, P = N_SSM_GROUPS, SSM_STATE
    return {
        'x_prompt': nrm(ks[0], (BATCH, SEQ, D_MODEL), 1.0),
        'x_sample': nrm(ks[1], (DEC_BATCH, DEC_SEQ, D_MODEL), 1.0),
        'cache_k': nrm(ks[2], (DEPTH, n_pool, PAGE_SIZE, N_HEADS, HEAD_DIM), 1.0),
        'cache_v': nrm(ks[3], (DEPTH, n_pool, PAGE_SIZE, N_HEADS, HEAD_DIM), 1.0),
        'cache_idx_k': nrm(ks[4], (DEPTH, n_pool, PAGE_SIZE, IDX_DIM), 1.0),
        'state_ssm_re': nrm(ks[5], (DEPTH, DEC_BATCH, G, P), 0.3),
        'state_ssm_im': nrm(ks[6], (DEPTH, DEC_BATCH, G, P), 0.3),
        'state_conv': nrm(ks[7], (DEPTH, DEC_BATCH, CONV_WIDTH - 1, D_CONV), 0.5),
        'page_table': page_table,
        'w_in': nrm(ks[9], (DEPTH, D_MODEL, D_IN), D_MODEL ** -0.5),
        'ssm_a_re': -0.5 + nrm(ks[10], (DEPTH, G, P), 0.01),
        'ssm_a_im': jnp.pi * jnp.arange(P, dtype=F32) + nrm(ks[11], (DEPTH, G, P), 0.01),
        'ssm_log_dt': jax.random.uniform(ks[12], (DEPTH, G), F32, math.log(0.001), math.log(0.1)),
        'ssm_b_re': nrm(ks[13], (DEPTH, G, P, SSM_GROUP), (2.0 * SSM_GROUP) ** -0.5),
        'ssm_b_im': nrm(ks[14], (DEPTH, G, P, SSM_GROUP), (2.0 * SSM_GROUP) ** -0.5),
        'ssm_c_re': nrm(ks[15], (DEPTH, G, SSM_GROUP, P), (2.0 * P) ** -0.5),
        'ssm_c_im': nrm(ks[16], (DEPTH, G, SSM_GROUP, P), (2.0 * P) ** -0.5),
        'ssm_d': nrm(ks[17], (DEPTH, D_SSM), 1.0),
        'ssm_w_glu': nrm(ks[18], (DEPTH, D_SSM, 2 * D_MODEL), D_SSM ** -0.5),
        'conv_w_dw': nrm(ks[19], (DEPTH, CONV_WIDTH, D_CONV), CONV_WIDTH ** -0.5),
        'conv_b_dw': nrm(ks[20], (DEPTH, D_CONV), 0.01),
        'conv_ln_g': 1.0 + nrm(ks[21], (DEPTH, D_CONV), 0.01),
        'conv_ln_b': nrm(ks[22], (DEPTH, D_CONV), 0.01),
        'conv_w_pw': nrm(ks[23], (DEPTH, D_CONV, D_MODEL), D_CONV ** -0.5),
        'attn_w_o': nrm(ks[24], (DEPTH, D_ATTN, D_MODEL), D_ATTN ** -0.5),
        'w_out': nrm(ks[25], (DEPTH, D_MODEL, D_MODEL), DN_BETA * D_MODEL ** -0.5),
        'ln1_g': 1.0 + nrm(ks[26], (DEPTH, D_MODEL), 0.01),
        'ln1_b': nrm(ks[27], (DEPTH, D_MODEL), 0.01),
        'ln2_g': 1.0 + nrm(ks[28], (DEPTH, D_MODEL), 0.01),
        'ln2_b': nrm(ks[29], (DEPTH, D_MODEL), 0.01),
        'ffn_w_gate': nrm(ks[30], (N_DENSE, D_MODEL, D_FF), D_MODEL ** -0.5),
        'ffn_w_up': nrm(ks[31], (N_DENSE, D_MODEL, D_FF), D_MODEL ** -0.5),
        'ffn_w_down': nrm(ks[32], (N_DENSE, D_FF, D_MODEL), DN_BETA * D_FF ** -0.5),
        'moe_w_router': nrm(ks[33], (N_MOE, D_MODEL, N_EXPERTS), D_MODEL ** -0.5),
        'moe_w_gate': nrm(ks[34], (N_MOE, N_EXPERTS, D_MODEL, D_FF), D_MODEL ** -0.5),
        'moe_w_up': nrm(ks[35], (N_MOE, N_EXPERTS, D_MODEL, D_FF), D_MODEL ** -0.5),
        'moe_w_down': nrm(ks[36], (N_MOE, N_EXPERTS, D_FF, D_MODEL), DN_BETA * D_FF ** -0.5),
    }


def reference(x_prompt, x_sample, cache_k, cache_v, cache_idx_k, state_ssm_re, state_ssm_im, state_conv,
              page_table, w_in, ssm_a_re, ssm_a_im, ssm_log_dt, ssm_b_re, ssm_b_im, ssm_c_re, ssm_c_im, ssm_d,
              ssm_w_glu, conv_w_dw, conv_b_dw, conv_ln_g, conv_ln_b, conv_w_pw, attn_w_o, w_out,
              ln1_g, ln1_b, ln2_g, ln2_b, ffn_w_gate, ffn_w_up, ffn_w_down,
              moe_w_router, moe_w_gate, moe_w_up, moe_w_down):
    xp, xs = x_prompt, x_sample
    n_p = xp.shape[0]
    h_zero = jnp.zeros((n_p, N_SSM_GROUPS, SSM_STATE), F32)
    buf_zero = jnp.zeros((n_p, CONV_WIDTH - 1, D_CONV), xp.dtype)
    kp_l, vp_l, kip_l, hpr_l, hpi_l, bp_l = [], [], [], [], [], []
    ks_l, vs_l, kis_l, hsr_l, hsi_l, bs_l = [], [], [], [], [], []
    for l in range(DEPTH):
        w_l = (w_in[l], ssm_a_re[l], ssm_a_im[l], ssm_log_dt[l], ssm_b_re[l], ssm_b_im[l], ssm_c_re[l],
               ssm_c_im[l], ssm_d[l], ssm_w_glu[l], conv_w_dw[l], conv_b_dw[l], conv_ln_g[l], conv_ln_b[l],
               conv_w_pw[l], attn_w_o[l], w_out[l])
        mp, hpr, hpi, bp, kp, vp, kip = _mixer(xp, h_zero, h_zero, buf_zero, _dsa_prompt, *w_l)
        attend_s = functools.partial(_dsa_sample, pool_k=cache_k[l], pool_v=cache_v[l],
                                     pool_ki=cache_idx_k[l], page_table=page_table)
        ms, hsr, hsi, bs, ks, vs, kis = _mixer(xs, state_ssm_re[l], state_ssm_im[l], state_conv[l], attend_s, *w_l)
        xp = _layer_norm(DN_ALPHA * xp + mp, ln1_g[l], ln1_b[l])
        xs = _layer_norm(DN_ALPHA * xs + ms, ln1_g[l], ln1_b[l])
        j = l // 2
        if l % 2 == 0:
            fp = _swiglu(xp, ffn_w_gate[j], ffn_w_up[j], ffn_w_down[j])
            fs = _swiglu(xs, ffn_w_gate[j], ffn_w_up[j], ffn_w_down[j])
        else:
            fp = _moe(xp, moe_w_router[j], moe_w_gate[j], moe_w_up[j], moe_w_down[j])
            fs = _moe(xs, moe_w_router[j], moe_w_gate[j], moe_w_up[j], moe_w_down[j])
        xp = _layer_norm(DN_ALPHA * xp + fp, ln2_g[l], ln2_b[l])
        xs = _layer_norm(DN_ALPHA * xs + fs, ln2_g[l], ln2_b[l])
        kp_l.append(kp); vp_l.append(vp); kip_l.append(kip); hpr_l.append(hpr); hpi_l.append(hpi); bp_l.append(bp)
        ks_l.append(ks); vs_l.append(vs); kis_l.append(kis); hsr_l.append(hsr); hsi_l.append(hsi); bs_l.append(bs)
    return (xp, xs,
            jnp.stack(kp_l), jnp.stack(vp_l), jnp.stack(kip_l), jnp.stack(hpr_l), jnp.stack(hpi_l), jnp.stack(bp_l),
            jnp.stack(ks_l), jnp.stack(vs_l), jnp.stack(kis_l), jnp.stack(hsr_l), jnp.stack(hsi_l), jnp.stack(bs_l))
```

```python
import functools

import jax
import jax.numpy as jnp
import numpy as np
from jax import lax
from jax.experimental import pallas as pl
from jax.experimental.pallas import tpu as pltpu

F32 = jnp.float32
BF16 = jnp.bfloat16
I32 = jnp.int32

D_MODEL = 1024
DEPTH = 4
PAGE_SIZE = 128
D_SSM = 512
SSM_GROUP = 16
N_SSM_GROUPS = 32
SSM_STATE = 64
N_STATE = N_SSM_GROUPS * SSM_STATE
D_CONV = 512
CONV_WIDTH = 31
N_HEADS = 8
HEAD_DIM = 64
D_ATTN = 512
N_IDX_HEADS = 8
IDX_DIM = 64
TOPK_MAX = 256
ATTN_SCALE = HEAD_DIM ** -0.5
D_FF = 2816
N_EXPERTS = 8
DN_ALPHA = (2.0 * DEPTH) ** 0.25
LN_EPS = 1e-5
C_U, C_ZC, C_Q, C_K, C_V, C_QI, C_KI, C_WI, C_GL, C_END = 0, 512, 1536, 2048, 2560, 3072, 3584, 3648, 3656, 6728

LANES = 128
SUBLANES = 8
VMEM_LIMIT_BYTES = 56 * 1024 * 1024

INT_MIN = -(2 ** 31)
NEG_BIG = -1e30


def _cparams(semantics=None):
    return pltpu.CompilerParams(dimension_semantics=semantics, vmem_limit_bytes=VMEM_LIMIT_BYTES)


def _layer_norm(x, g, b):
    mu = jnp.mean(x, axis=-1, keepdims=True)
    xc = x - mu
    var = jnp.mean(xc * xc, axis=-1, keepdims=True)
    return xc * lax.rsqrt(var + LN_EPS) * g + b


def _sortable_key(score):
    bits = lax.bitcast_convert_type(score, I32)
    return bits ^ ((bits >> 31) & 0x7FFFFFFF)


def _wrap_i32(v):
    return ((v + 2 ** 31) % 2 ** 32) - 2 ** 31


def _store_head_pairs(dst_ref, z):
    lo = lax.broadcasted_iota(I32, (z.shape[0], LANES), 1) < HEAD_DIM
    for j in range(N_HEADS // 2):
        pair = z[:, LANES * j:LANES * (j + 1)]
        dst_ref[:, 2 * LANES * j:2 * LANES * j + LANES] = jnp.where(lo, pair, 0.0).astype(BF16)
        dst_ref[:, 2 * LANES * j + LANES:2 * LANES * (j + 1)] = jnp.where(lo, 0.0, pair).astype(BF16)


def _in_proj_a_body(x_ref, w_ref, u_ref, zc_ref, q_ref, k_ref, v_ref, vb_ref, qi_ref, *, pad_heads):
    x = x_ref[...]

    def mm(c0, c1):
        return jnp.dot(x, w_ref[:, c0:c1], preferred_element_type=F32)

    u_ref[...] = mm(C_U, C_ZC)
    zc_ref[...] = mm(C_ZC, C_Q)
    k_ref[...] = mm(C_K, C_V)
    v = mm(C_V, C_QI)
    v_ref[...] = v
    vb_ref[...] = v.astype(BF16)
    q = mm(C_Q, C_K) * ATTN_SCALE
    qi = mm(C_QI, C_KI)
    if pad_heads:
        _store_head_pairs(q_ref, q)
        _store_head_pairs(qi_ref, qi)
    else:
        q_ref[...] = q
        qi_ref[...] = qi


def _in_proj_a(xb, w_a, *, pad_heads, tm):
    m = xb.shape[0]
    row = lambda i: (i, 0)
    qshape = (jax.ShapeDtypeStruct((m, 2 * D_ATTN), BF16) if pad_heads else jax.ShapeDtypeStruct((m, D_ATTN), F32))
    qw = qshape.shape[1]
    return pl.pallas_call(
        functools.partial(_in_proj_a_body, pad_heads=pad_heads),
        grid=(m // tm,),
        in_specs=[pl.BlockSpec((tm, D_MODEL), row), pl.BlockSpec((D_MODEL, C_KI), lambda i: (0, 0))],
        out_specs=[pl.BlockSpec((tm, D_SSM), row), pl.BlockSpec((tm, 2 * D_CONV), row), pl.BlockSpec((tm, qw), row),
                   pl.BlockSpec((tm, D_ATTN), row), pl.BlockSpec((tm, D_ATTN), row), pl.BlockSpec((tm, D_ATTN), row),
                   pl.BlockSpec((tm, qw), row)],
        out_shape=[jax.ShapeDtypeStruct((m, D_SSM), F32), jax.ShapeDtypeStruct((m, 2 * D_CONV), F32), qshape,
                   jax.ShapeDtypeStruct((m, D_ATTN), F32), jax.ShapeDtypeStruct((m, D_ATTN), F32),
                   jax.ShapeDtypeStruct((m, D_ATTN), BF16), qshape],
        compiler_params=_cparams(("arbitrary",)),
        name="in_proj_a",
    )(xb, w_a)


def _in_proj_b_body(x_ref, w_ref, gl_ref, ki_ref, wi_ref):
    x = x_ref[...]
    n_gl = 3 * D_MODEL
    gl_ref[...] = jnp.dot(x, w_ref[:, 0:n_gl], preferred_element_type=F32)
    ki_ref[...] = jnp.dot(x, w_ref[:, n_gl:n_gl + LANES], preferred_element_type=F32)[:, :IDX_DIM]
    wi_ref[...] = jnp.dot(x, w_ref[:, n_gl + LANES:n_gl + 2 * LANES], preferred_element_type=F32)[:, :N_IDX_HEADS]


def _in_proj_b(xb, w_b, *, tm):
    m = xb.shape[0]
    row = lambda i: (i, 0)
    n_gl = 3 * D_MODEL
    return pl.pallas_call(
        _in_proj_b_body,
        grid=(m // tm,),
        in_specs=[pl.BlockSpec((tm, D_MODEL), row), pl.BlockSpec((D_MODEL, n_gl + 2 * LANES), lambda i: (0, 0))],
        out_specs=[pl.BlockSpec((tm, n_gl), row), pl.BlockSpec((tm, IDX_DIM), row), pl.BlockSpec((tm, N_IDX_HEADS), row)],
        out_shape=[jax.ShapeDtypeStruct((m, n_gl), F32), jax.ShapeDtypeStruct((m, IDX_DIM), F32),
                   jax.ShapeDtypeStruct((m, N_IDX_HEADS), F32)],
        compiler_params=_cparams(("arbitrary",)),
        name="in_proj_b",
    )(xb, w_b)


def _s5_readout(h_bf16_re, h_bf16_im, u, ccat_ref, d_ref, wglu_ref):
    y = (jnp.dot(h_bf16_re, ccat_ref[0:N_STATE, :], preferred_element_type=F32)
         + jnp.dot(h_bf16_im, ccat_ref[N_STATE:2 * N_STATE, :], preferred_element_type=F32)
         + d_ref[...] * u)
    y = jax.nn.gelu(y)
    y2 = jnp.dot(y.astype(BF16), wglu_ref[...], preferred_element_type=F32)
    return y2[:, :D_MODEL] * jax.nn.sigmoid(y2[:, D_MODEL:])


def _s5_prompt_body(u_ref, bblk_ref, ab_ref, ccat_ref, d_ref, wglu_ref, ya_ref, hfin_ref, xs_sc, hc_sc, pw_sc,
                    *, tt, nseg):
    ti = pl.program_id(1)
    seg = tt // nseg
    ns = N_STATE
    n_ct = ns // LANES
    ar = ab_ref[0:1, :]
    ai = ab_ref[1:2, :]

    @pl.when(ti == 0)
    def _():
        hc_sc[...] = jnp.zeros_like(hc_sc)

        def pw_body(j, carry):
            pr, pi = carry
            pw_sc[pl.ds(j, 1), 0:ns] = pr
            pw_sc[pl.ds(j, 1), ns:2 * ns] = pi
            return pr * ar - pi * ai, pr * ai + pi * ar

        lax.fori_loop(0, seg, pw_body, (ar, ai))

    u = u_ref[...]
    x = jnp.dot(u.astype(BF16), bblk_ref[...], preferred_element_type=F32)
    for ct in range(2 * n_ct):
        xs_sc[ct] = x[:, LANES * ct:LANES * (ct + 1)]

    def load_rows(rows, base):
        return jnp.concatenate([xs_sc[base + ct, rows, :] for ct in range(n_ct)], axis=1)

    def store_rows(rows, base, val):
        for ct in range(n_ct):
            xs_sc[base + ct, rows, :] = val[:, LANES * ct:LANES * (ct + 1)]

    ar8 = jnp.broadcast_to(ar, (nseg, ns))
    ai8 = jnp.broadcast_to(ai, (nseg, ns))

    def local_scan(j, carry):
        hr, hi = carry
        rows = pl.ds(j, nseg, stride=seg)
        nr = ar8 * hr - ai8 * hi + load_rows(rows, 0)
        ni = ar8 * hi + ai8 * hr + load_rows(rows, n_ct)
        store_rows(rows, 0, nr)
        store_rows(rows, n_ct, ni)
        return nr, ni

    zero = jnp.zeros((nseg, ns), F32)
    lax.fori_loop(0, seg, local_scan, (zero, zero))

    ends = pl.ds(seg - 1, nseg, stride=seg)
    er = load_rows(ends, 0)
    ei = load_rows(ends, n_ct)
    asr = pw_sc[seg - 1:seg, 0:ns]
    asi = pw_sc[seg - 1:seg, ns:2 * ns]
    cr = hc_sc[0:1, :]
    ci = hc_sc[1:2, :]
    crs, cis = [], []
    for s in range(nseg):
        crs.append(cr)
        cis.append(ci)
        cr, ci = asr * cr - asi * ci + er[s:s + 1], asr * ci + asi * cr + ei[s:s + 1]
    hc_sc[0:1, :] = cr
    hc_sc[1:2, :] = ci
    c8r = jnp.concatenate(crs, axis=0)
    c8i = jnp.concatenate(cis, axis=0)

    def add_carry(j, _):
        rows = pl.ds(j, nseg, stride=seg)
        pr = pw_sc[pl.ds(j, 1), 0:ns]
        pi = pw_sc[pl.ds(j, 1), ns:2 * ns]
        store_rows(rows, 0, load_rows(rows, 0) + (pr * c8r - pi * c8i))
        store_rows(rows, n_ct, load_rows(rows, n_ct) + (pr * c8i + pi * c8r))
        return 0

    lax.fori_loop(0, seg, add_carry, 0)

    hfin_ref[...] = hc_sc[...]
    h_re = jnp.concatenate([xs_sc[ct].astype(BF16) for ct in range(n_ct)], axis=1)
    h_im = jnp.concatenate([xs_sc[n_ct + ct].astype(BF16) for ct in range(n_ct)], axis=1)
    ya_ref[...] = _s5_readout(h_re, h_im, u, ccat_ref, d_ref, wglu_ref)


def _s5_prompt(u, s5w, *, n_batch, t, tt, nseg=SUBLANES):
    bblk, ab, ccat, d, wglu = s5w
    nt = t // tt
    const = lambda b, i: (0, 0)
    return pl.pallas_call(
        functools.partial(_s5_prompt_body, tt=tt, nseg=nseg),
        grid=(n_batch, nt),
        in_specs=[pl.BlockSpec((tt, D_SSM), lambda b, i: (b * nt + i, 0)),
                  pl.BlockSpec(bblk.shape, const), pl.BlockSpec(ab.shape, const), pl.BlockSpec(ccat.shape, const),
                  pl.BlockSpec(d.shape, const), pl.BlockSpec(wglu.shape, const)],
        out_specs=[pl.BlockSpec((tt, D_MODEL), lambda b, i: (b * nt + i, 0)),
                   pl.BlockSpec((None, 2, N_STATE), lambda b, i: (b, 0, 0))],
        out_shape=[jax.ShapeDtypeStruct((n_batch * t, D_MODEL), F32), jax.ShapeDtypeStruct((n_batch, 2, N_STATE), F32)],
        scratch_shapes=[pltpu.VMEM((2 * N_STATE // LANES, tt, LANES), F32), pltpu.VMEM((2, N_STATE), F32),
                        pltpu.VMEM((tt // nseg, 2 * N_STATE), F32)],
        compiler_params=_cparams(("arbitrary", "arbitrary")),
        name="s5_prompt",
    )(u, bblk, ab, ccat, d, wglu)


def _s5_sample_body(u_ref, h0r_ref, h0i_ref, bblk_ref, ab_ref, ccat_ref, d_ref, wglu_ref, ya_ref, hr_ref, hi_ref,
                    *, n, t):
    ns = N_STATE
    ar = ab_ref[0:1, :]
    ai = ab_ref[1:2, :]
    hr = h0r_ref[...]
    hi = h0i_ref[...]
    for step in range(t):
        u = u_ref[step]
        x = jnp.dot(u.astype(BF16), bblk_ref[...], preferred_element_type=F32)
        hr, hi = ar * hr - ai * hi + x[:, 0:ns], ar * hi + ai * hr + x[:, ns:2 * ns]
        ya_ref[step] = _s5_readout(hr.astype(BF16), hi.astype(BF16), u, ccat_ref, d_ref, wglu_ref)
    hr_ref[...] = hr
    hi_ref[...] = hi


def _s5_sample(u, h0r, h0i, s5w, *, n, t):
    bblk, ab, ccat, d, wglu = s5w
    return pl.pallas_call(
        functools.partial(_s5_sample_body, n=n, t=t),
        out_shape=[jax.ShapeDtypeStruct((t, n, D_MODEL), F32), jax.ShapeDtypeStruct((n, N_STATE), F32),
                   jax.ShapeDtypeStruct((n, N_STATE), F32)],
        compiler_params=_cparams(),
        name="s5_sample",
    )(u, h0r, h0i, bblk, ab, ccat, d, wglu)


CONV_HALO = 32
CONV_ROWS = 64


def _conv_tail(y, bdw_ref, lng_ref, lnb_ref, wpw_ref):
    y = _layer_norm(y + bdw_ref[...], lng_ref[...], lnb_ref[...])
    y = y * jax.nn.sigmoid(y)
    return jnp.dot(y.astype(BF16), wpw_ref[...], preferred_element_type=F32)


def _conv_prompt_body(zc_ref, wdw_ref, bdw_ref, lng_ref, lnb_ref, wpw_ref, yb_ref, nb_ref, vb_sc, y_sc, *, tt):
    ti = pl.program_id(1)
    halo = CONV_HALO
    off = halo - (CONV_WIDTH - 1)

    @pl.when(ti == 0)
    def _():
        vb_sc[0:halo, :] = jnp.zeros((halo, D_CONV), F32)

    zc = zc_ref[...]
    vb_sc[halo:halo + tt, :] = zc[:, :D_CONV] * jax.nn.sigmoid(zc[:, D_CONV:])
    for r0 in range(0, tt, CONV_ROWS):
        acc = jnp.zeros((CONV_ROWS, D_CONV), F32)
        for j in range(CONV_WIDTH):
            acc = acc + wdw_ref[j:j + 1, :] * vb_sc[r0 + j + off:r0 + j + off + CONV_ROWS, :]
        y_sc[r0:r0 + CONV_ROWS, :] = acc
    yb_ref[...] = _conv_tail(y_sc[...], bdw_ref, lng_ref, lnb_ref, wpw_ref)
    tail = vb_sc[tt:tt + halo, :]
    nb_ref[...] = tail
    vb_sc[0:halo, :] = tail


def _conv_prompt(zc, cw, *, n_batch, t, tt):
    wdw, bdw, lng, lnb, wpw = cw
    nt = t // tt
    const = lambda b, i: (0, 0)
    return pl.pallas_call(
        functools.partial(_conv_prompt_body, tt=tt),
        grid=(n_batch, nt),
        in_specs=[pl.BlockSpec((tt, 2 * D_CONV), lambda b, i: (b * nt + i, 0)),
                  pl.BlockSpec(wdw.shape, const), pl.BlockSpec(bdw.shape, const), pl.BlockSpec(lng.shape, const),
                  pl.BlockSpec(lnb.shape, const), pl.BlockSpec(wpw.shape, const)],
        out_specs=[pl.BlockSpec((tt, D_MODEL), lambda b, i: (b * nt + i, 0)),
                   pl.BlockSpec((None, CONV_HALO, D_CONV), lambda b, i: (b, 0, 0))],
        out_shape=[jax.ShapeDtypeStruct((n_batch * t, D_MODEL), F32),
                   jax.ShapeDtypeStruct((n_batch, CONV_HALO, D_CONV), F32)],
        scratch_shapes=[pltpu.VMEM((tt + CONV_HALO, D_CONV), F32), pltpu.VMEM((tt, D_CONV), F32)],
        compiler_params=_cparams(("arbitrary", "arbitrary")),
        name="conv_prompt",
    )(zc, wdw, bdw, lng, lnb, wpw)


def _conv_sample_body(zc_ref, buf_ref, wdw_ref, bdw_ref, lng_ref, lnb_ref, wpw_ref, yb_ref, nb_ref, vp_sc, *, n, t):
    nbuf = CONV_WIDTH - 1
    vp_sc[0:nbuf] = buf_ref[...]
    for step in range(t):
        zc = zc_ref[step]
        vp_sc[nbuf + step] = zc[:, :D_CONV] * jax.nn.sigmoid(zc[:, D_CONV:])
    half = D_CONV // 2
    for step in range(t):
        cols = []
        for c0 in range(0, D_CONV, half):
            acc = jnp.zeros((n, half), F32)
            for j in range(CONV_WIDTH):
                acc = acc + wdw_ref[j:j + 1, c0:c0 + half] * vp_sc[step + j, :, c0:c0 + half]
            cols.append(acc)
        y = jnp.concatenate(cols, axis=1)
        yb_ref[step] = _conv_tail(y, bdw_ref, lng_ref, lnb_ref, wpw_ref)
    nb_ref[...] = vp_sc[t:t + nbuf]


def _conv_sample(zc, buf, cw, *, n, t):
    wdw, bdw, lng, lnb, wpw = cw
    nbuf = CONV_WIDTH - 1
    return pl.pallas_call(
        functools.partial(_conv_sample_body, n=n, t=t),
        out_shape=[jax.ShapeDtypeStruct((t, n, D_MODEL), F32), jax.ShapeDtypeStruct((nbuf, n, D_CONV), F32)],
        scratch_shapes=[pltpu.VMEM((nbuf + t, n, D_CONV), F32)],
        compiler_params=_cparams(),
        name="conv_sample",
    )(zc, buf, wdw, bdw, lng, lnb, wpw)


def _attn_prompt_body(qp_ref, qip_ref, wi_ref, kt_ref, v_ref, kit_ref, o_ref, keys_sc, m_sc, l_sc, acc_sc,
                      *, tq, tk, n_kblk, k_sel):
    i = pl.program_id(1)
    n_ch = (i * tq + tq - 1) // tk + 1
    row = lax.broadcasted_iota(I32, (tq, tk), 0)
    lane = lax.broadcasted_iota(I32, (tq, tk), 1)
    qpos = i * tq + row
    w = wi_ref[...]

    def score_chunk(c, _):
        kic = kit_ref[c]
        acc = jnp.zeros((tq, tk), F32)
        for h in range(N_IDX_HEADS):
            d = jnp.dot(qip_ref[:, LANES * h:LANES * (h + 1)], kic, preferred_element_type=F32)
            acc = acc + w[:, h:h + 1] * jnp.maximum(d, 0.0)
        keys_sc[c] = jnp.where(c * tk + lane <= qpos, _sortable_key(acc), INT_MIN)
        return 0

    lax.fori_loop(0, n_ch, score_chunk, 0)

    def count(pred):
        def body(c, part):
            hit = jnp.where(pred(c, keys_sc[c]), 1.0, 0.0)
            for s in range(tk // LANES):
                part = part + hit[:, LANES * s:LANES * (s + 1)]
            return part

        part = lax.fori_loop(0, n_ch, body, jnp.zeros((tq, LANES), F32))
        return jnp.sum(part, axis=1, keepdims=True)

    def count_ge(cand):
        return count(lambda c, kk: kk >= cand)

    def bit_body(b, t):
        cand = t + lax.shift_left(jnp.int32(1), 31 - b)
        return jnp.where(count_ge(cand) >= k_sel, cand, t)

    t = lax.fori_loop(0, 32, bit_body, jnp.full((tq, 1), INT_MIN, I32))

    cnt_gt = count_ge(t + 1)
    need = k_sel - cnt_gt
    tie = ((count_ge(t) - cnt_gt) > need) & (t > INT_MIN)

    @pl.when(jnp.max(jnp.where(tie, 1.0, 0.0)) > 0.0)
    def _():
        nbits = (n_kblk * tk - 1).bit_length()

        def col_body(b, p):
            cand = p + lax.shift_left(jnp.int32(1), nbits - 1 - b)
            cnt = count(lambda c, kk: (kk == t) & (c * tk + lane < cand))
            return jnp.where(cnt < need, cand, p)

        last = lax.fori_loop(0, nbits, col_body, jnp.zeros((tq, 1), I32))

        def drop(c, _):
            kk = keys_sc[c]
            keys_sc[c] = jnp.where(tie & (kk == t) & (c * tk + lane > last), INT_MIN, kk)
            return 0

        lax.fori_loop(0, n_ch, drop, 0)

    t_sel = jnp.maximum(t, INT_MIN + 1)

    m_sc[...] = jnp.full(m_sc.shape, NEG_BIG, F32)
    l_sc[...] = jnp.zeros(l_sc.shape, F32)
    acc_sc[...] = jnp.zeros(acc_sc.shape, F32)
    lo = lax.broadcasted_iota(I32, (tq, LANES), 1) < HEAD_DIM

    def attend_chunk(c, _):
        bias = jnp.where(keys_sc[c] >= t_sel, 0.0, NEG_BIG)
        for j in range(N_HEADS // 2):
            ktj = kt_ref[c, LANES * j:LANES * (j + 1), :]
            vj = v_ref[c, :, LANES * j:LANES * (j + 1)]
            pv, al = [], []
            for half in range(2):
                h = 2 * j + half
                s = jnp.dot(qp_ref[:, LANES * h:LANES * (h + 1)], ktj, preferred_element_type=F32) + bias
                m_prev = m_sc[h]
                m_new = jnp.maximum(m_prev, jnp.max(s, axis=1, keepdims=True))
                alpha = jnp.exp(m_prev - m_new)
                p = jnp.exp(s - m_new)
                l_sc[h] = alpha * l_sc[h] + jnp.sum(p, axis=1, keepdims=True)
                m_sc[h] = m_new
                pv.append(jnp.dot(p.astype(BF16), vj, preferred_element_type=F32))
                al.append(alpha)
            acc_sc[j] = acc_sc[j] * jnp.where(lo, al[0], al[1]) + jnp.where(lo, pv[0], pv[1])
        return 0

    lax.fori_loop(0, n_ch, attend_chunk, 0)
    for j in range(N_HEADS // 2):
        inv = jnp.where(lo, 1.0 / l_sc[2 * j], 1.0 / l_sc[2 * j + 1])
        o_ref[:, LANES * j:LANES * (j + 1)] = (acc_sc[j] * inv).astype(BF16)


def _attn_prompt(qp, qip, wi, kt, vb, kit, *, n_batch, t, tq, tk):
    nq = t // tq
    nk = t // tk
    k_sel = min(TOPK_MAX, t // 4)
    qrow = lambda b, i: (b * nq + i, 0)
    whole = lambda b, i: (b, 0, 0, 0)
    return pl.pallas_call(
        functools.partial(_attn_prompt_body, tq=tq, tk=tk, n_kblk=nk, k_sel=k_sel),
        grid=(n_batch, nq),
        in_specs=[pl.BlockSpec((tq, 2 * D_ATTN), qrow), pl.BlockSpec((tq, 2 * D_ATTN), qrow),
                  pl.BlockSpec((tq, N_IDX_HEADS), qrow),
                  pl.BlockSpec((None, nk, D_ATTN, tk), whole, pipeline_mode=pl.Buffered(1)),
                  pl.BlockSpec((None, nk, tk, D_ATTN), whole, pipeline_mode=pl.Buffered(1)),
                  pl.BlockSpec((None, nk, LANES, tk), whole, pipeline_mode=pl.Buffered(1))],
        out_specs=pl.BlockSpec((tq, D_ATTN), qrow),
        out_shape=jax.ShapeDtypeStruct((n_batch * t, D_ATTN), BF16),
        scratch_shapes=[pltpu.VMEM((nk, tq, tk), I32), pltpu.VMEM((N_HEADS, tq, 1), F32),
                        pltpu.VMEM((N_HEADS, tq, 1), F32), pltpu.VMEM((N_HEADS // 2, tq, LANES), F32)],
        compiler_params=_cparams(("arbitrary", "arbitrary")),
        name="attn_prompt",
    )(qp, qip, wi, kt, vb, kit)


SAMPLE_RADIX_BITS = 4


def _attn_sample_body(pt_ref, q_ref, qi_ref, wi_ref, kn_ref, vn_ref, kin_ref, *rest, n_pages, t, k_sel):
    del pt_ref
    ki_pages = rest[0:n_pages]
    k_pages = rest[n_pages:2 * n_pages]
    v_pages = rest[2 * n_pages:3 * n_pages]
    o_ref, kib_sc, kb_sc, vb_sc, keys_sc = rest[3 * n_pages:]
    past = n_pages * PAGE_SIZE
    nkp = past + PAGE_SIZE
    rows = N_HEADS * t

    for j in range(n_pages):
        sl = slice(PAGE_SIZE * j, PAGE_SIZE * (j + 1))
        kib_sc[sl, :] = ki_pages[j][...].astype(BF16)
        kb_sc[sl, :] = k_pages[j][...].astype(BF16)
        vb_sc[sl, :] = v_pages[j][...].astype(BF16)

    def tail(new):
        return jnp.concatenate([new, jnp.zeros((PAGE_SIZE - t, new.shape[1]), F32)], axis=0).astype(BF16)

    kib_sc[past:nkp, :] = tail(kin_ref[...])
    kb_sc[past:nkp, :] = tail(kn_ref[...])
    vb_sc[past:nkp, :] = tail(vn_ref[...])

    r_i = lax.broadcasted_iota(I32, (rows, D_ATTN), 0)
    l_i = lax.broadcasted_iota(I32, (rows, D_ATTN), 1)
    own = (l_i // HEAD_DIM) == (r_i // t)
    qblk = jnp.where(own, jnp.concatenate([q_ref[...]] * N_HEADS, axis=0), 0.0).astype(BF16)
    qiblk = jnp.where(own, jnp.concatenate([qi_ref[...]] * N_IDX_HEADS, axis=0), 0.0).astype(BF16)
    fr = lax.broadcasted_iota(I32, (D_ATTN, IDX_DIM), 0)
    fc = lax.broadcasted_iota(I32, (D_ATTN, IDX_DIM), 1)
    fold = jnp.where((fr % IDX_DIM) == fc, 1.0, 0.0).astype(BF16)
    qi_rows = jnp.dot(qiblk, fold, preferred_element_type=F32).astype(BF16)

    nt = (((1,), (1,)), ((), ()))
    dots = lax.dot_general(qi_rows, kib_sc[...], nt, preferred_element_type=F32)
    w = wi_ref[...]
    score = jnp.zeros((t, nkp), F32)
    for h in range(N_IDX_HEADS):
        score = score + w[:, h:h + 1] * jnp.maximum(dots[t * h:t * (h + 1), :], 0.0)
    col = lax.broadcasted_iota(I32, (t, nkp), 1)
    qpos = past + lax.broadcasted_iota(I32, (t, nkp), 0)
    keys = jnp.where(col <= qpos, _sortable_key(score), INT_MIN)
    keys_sc[...] = keys

    def count(mask):
        return jnp.sum(jnp.where(mask, 1.0, 0.0), axis=1, keepdims=True)

    rb = SAMPLE_RADIX_BITS
    thr = jnp.full((t, 1), INT_MIN, I32)
    for p in range(32 // rb):
        shift = 32 - rb * (p + 1)
        inc = jnp.zeros((t, 1), I32)
        for jj in range(1, 2 ** rb):
            cand = thr + jnp.int32(_wrap_i32(jj << shift))
            inc = inc + jnp.where(count(keys >= cand) >= k_sel, 1, 0).astype(I32)
        thr = thr + lax.shift_left(inc, jnp.int32(shift))

    cnt_gt = count(keys >= thr + 1)
    need = k_sel - cnt_gt
    tie = ((count(keys >= thr) - cnt_gt) > need) & (thr > INT_MIN)

    @pl.when(jnp.max(jnp.where(tie, 1.0, 0.0)) > 0.0)
    def _():
        nbits = (nkp - 1).bit_length()
        last = jnp.zeros((t, 1), I32)
        for b in range(nbits):
            cand = last + (1 << (nbits - 1 - b))
            last = jnp.where(count((keys == thr) & (col < cand)) < need, cand, last)
        keys_sc[...] = jnp.where(tie & (keys == thr) & (col > last), INT_MIN, keys)

    bias = jnp.where(keys_sc[...] >= jnp.maximum(thr, INT_MIN + 1), 0.0, NEG_BIG)
    s = lax.dot_general(qblk, kb_sc[...], nt, preferred_element_type=F32) + jnp.concatenate([bias] * N_HEADS, axis=0)
    m = jnp.max(s, axis=1, keepdims=True)
    p = jnp.exp(s - m)
    inv_l = 1.0 / jnp.sum(p, axis=1, keepdims=True)
    o_full = jnp.dot(p.astype(BF16), vb_sc[...], preferred_element_type=F32) * inv_l
    l_o = lax.broadcasted_iota(I32, (t, D_ATTN), 1)
    out = jnp.zeros((t, D_ATTN), F32)
    for h in range(N_HEADS):
        out = out + jnp.where((l_o // HEAD_DIM) == h, o_full[t * h:t * (h + 1), :], 0.0)
    o_ref[...] = out.astype(BF16)


def _attn_sample(page_table, q, qi, wi, kn, vn, kin, pool_ki, pool_k, pool_v, layer, *, n, t):
    n_pages = page_table.shape[1]
    past = n_pages * PAGE_SIZE
    nkp = past + PAGE_SIZE
    k_sel = min(TOPK_MAX, (past + t) // 4)
    seq = lambda i, pt: (i, 0, 0)

    def page_spec(c, j):
        return pl.BlockSpec((None, None, PAGE_SIZE, c), lambda i, pt, j=j: (layer, pt[i, j], 0, 0))

    in_specs = [pl.BlockSpec((None, t, D_ATTN), seq), pl.BlockSpec((None, t, D_ATTN), seq),
                pl.BlockSpec((None, t, N_IDX_HEADS), seq), pl.BlockSpec((None, t, D_ATTN), seq),
                pl.BlockSpec((None, t, D_ATTN), seq), pl.BlockSpec((None, t, IDX_DIM), seq)]
    in_specs += [page_spec(IDX_DIM, j) for j in range(n_pages)]
    in_specs += [page_spec(D_ATTN, j) for j in range(n_pages)]
    in_specs += [page_spec(D_ATTN, j) for j in range(n_pages)]
    grid_spec = pltpu.PrefetchScalarGridSpec(
        num_scalar_prefetch=1, grid=(n,), in_specs=in_specs,
        out_specs=pl.BlockSpec((None, t, D_ATTN), seq),
        scratch_shapes=[pltpu.VMEM((nkp, IDX_DIM), BF16), pltpu.VMEM((nkp, D_ATTN), BF16),
                        pltpu.VMEM((nkp, D_ATTN), BF16), pltpu.VMEM((t, nkp), I32)])
    return pl.pallas_call(
        functools.partial(_attn_sample_body, n_pages=n_pages, t=t, k_sel=k_sel),
        grid_spec=grid_spec,
        out_shape=jax.ShapeDtypeStruct((n, t, D_ATTN), BF16),
        compiler_params=_cparams(("arbitrary",)),
        name="attn_sample",
    )(page_table, q, qi, wi, kn, vn, kin, *([pool_ki] * n_pages), *([pool_k] * n_pages), *([pool_v] * n_pages))


def _merge_body(x_ref, ya_ref, yb_ref, at_ref, gl_ref, wo_ref, wout_ref, g_ref, b_ref, xo_ref, xob_ref):
    yc = jnp.dot(at_ref[...], wo_ref[...], preferred_element_type=F32)
    gl = gl_ref[...]
    merged = (jax.nn.sigmoid(gl[:, 0:D_MODEL]) * ya_ref[...]
              + jax.nn.sigmoid(gl[:, D_MODEL:2 * D_MODEL]) * yb_ref[...]
              + jax.nn.sigmoid(gl[:, 2 * D_MODEL:3 * D_MODEL]) * yc)
    mix = jnp.dot(merged.astype(BF16), wout_ref[...], preferred_element_type=F32)
    xn = _layer_norm(DN_ALPHA * x_ref[...] + mix, g_ref[...], b_ref[...])
    xo_ref[...] = xn
    xob_ref[...] = xn.astype(BF16)


def _merge(x, ya, yb, at, gl, wo, wout, g, b, *, tm):
    m = x.shape[0]
    row = lambda i: (i, 0)
    const = lambda i: (0, 0)
    return pl.pallas_call(
        _merge_body,
        grid=(m // tm,),
        in_specs=[pl.BlockSpec((tm, D_MODEL), row), pl.BlockSpec((tm, D_MODEL), row), pl.BlockSpec((tm, D_MODEL), row),
                  pl.BlockSpec((tm, D_ATTN), row), pl.BlockSpec((tm, 3 * D_MODEL), row),
                  pl.BlockSpec(wo.shape, const), pl.BlockSpec(wout.shape, const),
                  pl.BlockSpec(g.shape, const), pl.BlockSpec(b.shape, const)],
        out_specs=[pl.BlockSpec((tm, D_MODEL), row), pl.BlockSpec((tm, D_MODEL), row)],
        out_shape=[jax.ShapeDtypeStruct((m, D_MODEL), F32), jax.ShapeDtypeStruct((m, D_MODEL), BF16)],
        compiler_params=_cparams(("arbitrary",)),
        name="merge",
    )(x, ya, yb, at, gl, wo, wout, g, b)


def _swiglu_part(xb, wg, wu, wd):
    g = jnp.dot(xb, wg, preferred_element_type=F32)
    h = (g * jax.nn.sigmoid(g)) * jnp.dot(xb, wu, preferred_element_type=F32)
    return jnp.dot(h.astype(BF16), wd, preferred_element_type=F32)


def _finish_layer(x_ref, acc, g_ref, b_ref, xo_ref, xob_ref):
    xn = _layer_norm(DN_ALPHA * x_ref[...] + acc, g_ref[...], b_ref[...])
    xo_ref[...] = xn
    xob_ref[...] = xn.astype(BF16)


def _ffn_body(x_ref, xb_ref, wg_ref, wu_ref, wd_ref, g_ref, b_ref, xo_ref, xob_ref, acc_sc):
    f = pl.program_id(1)
    part = _swiglu_part(xb_ref[...], wg_ref[...], wu_ref[...], wd_ref[...])

    @pl.when(f == 0)
    def _():
        acc_sc[...] = part

    @pl.when(f > 0)
    def _():
        acc_sc[...] = acc_sc[...] + part

    @pl.when(f == pl.num_programs(1) - 1)
    def _():
        _finish_layer(x_ref, acc_sc[...], g_ref, b_ref, xo_ref, xob_ref)


def _ffn(x, xb, wg, wu, wd, g, b, *, tm, tf):
    m = x.shape[0]
    row = lambda i, f: (i, 0)
    const = lambda i, f: (0, 0)
    return pl.pallas_call(
        _ffn_body,
        grid=(m // tm, D_FF // tf),
        in_specs=[pl.BlockSpec((tm, D_MODEL), row), pl.BlockSpec((tm, D_MODEL), row),
                  pl.BlockSpec((D_MODEL, tf), lambda i, f: (0, f)), pl.BlockSpec((D_MODEL, tf), lambda i, f: (0, f)),
                  pl.BlockSpec((tf, D_MODEL), lambda i, f: (f, 0)),
                  pl.BlockSpec(g.shape, const), pl.BlockSpec(b.shape, const)],
        out_specs=[pl.BlockSpec((tm, D_MODEL), row), pl.BlockSpec((tm, D_MODEL), row)],
        out_shape=[jax.ShapeDtypeStruct((m, D_MODEL), F32), jax.ShapeDtypeStruct((m, D_MODEL), BF16)],
        scratch_shapes=[pltpu.VMEM((tm, D_MODEL), F32)],
        compiler_params=_cparams(("arbitrary", "arbitrary")),
        name="ffn",
    )(x, xb, wg, wu, wd, g, b)


def _moe_body(x_ref, xb_ref, wr_ref, wg_ref, wu_ref, wd_ref, g_ref, b_ref, xo_ref, xob_ref, acc_sc, gate_sc):
    e = pl.program_id(1)
    f = pl.program_id(2)
    tm = x_ref.shape[0]
    ids = lax.broadcasted_iota(I32, (tm, N_EXPERTS), 1).astype(F32)

    @pl.when((e == 0) & (f == 0))
    def _():
        logits = jnp.dot(x_ref[...], wr_ref[...], preferred_element_type=F32, precision=lax.Precision.HIGHEST)
        m1 = jnp.max(logits, axis=1, keepdims=True)
        i1 = jnp.min(jnp.where(logits == m1, ids, float(N_EXPERTS)), axis=1, keepdims=True)
        rest = jnp.where(ids == i1, -jnp.inf, logits)
        m2 = jnp.max(rest, axis=1, keepdims=True)
        i2 = jnp.min(jnp.where(rest == m2, ids, float(N_EXPERTS)), axis=1, keepdims=True)
        e2 = jnp.exp(m2 - m1)
        p1 = 1.0 / (1.0 + e2)
        gate_sc[...] = jnp.where(ids == i1, p1, 0.0) + jnp.where(ids == i2, e2 * p1, 0.0)
        acc_sc[...] = jnp.zeros_like(acc_sc)

    gate = jnp.sum(jnp.where(ids == e.astype(F32), gate_sc[...], 0.0), axis=1, keepdims=True)
    acc_sc[...] = acc_sc[...] + gate * _swiglu_part(xb_ref[...], wg_ref[...], wu_ref[...], wd_ref[...])

    @pl.when((e == pl.num_programs(1) - 1) & (f == pl.num_programs(2) - 1))
    def _():
        _finish_layer(x_ref, acc_sc[...], g_ref, b_ref, xo_ref, xob_ref)


def _moe(x, xb, wr, wg, wu, wd, g, b, *, tm, tf):
    m = x.shape[0]
    row = lambda i, e, f: (i, 0)
    const = lambda i, e, f: (0, 0)
    return pl.pallas_call(
        _moe_body,
        grid=(m // tm, N_EXPERTS, D_FF // tf),
        in_specs=[pl.BlockSpec((tm, D_MODEL), row), pl.BlockSpec((tm, D_MODEL), row),
                  pl.BlockSpec(wr.shape, const),
                  pl.BlockSpec((None, D_MODEL, tf), lambda i, e, f: (e, 0, f)),
                  pl.BlockSpec((None, D_MODEL, tf), lambda i, e, f: (e, 0, f)),
                  pl.BlockSpec((None, tf, D_MODEL), lambda i, e, f: (e, f, 0)),
                  pl.BlockSpec(g.shape, const), pl.BlockSpec(b.shape, const)],
        out_specs=[pl.BlockSpec((tm, D_MODEL), row), pl.BlockSpec((tm, D_MODEL), row)],
        out_shape=[jax.ShapeDtypeStruct((m, D_MODEL), F32), jax.ShapeDtypeStruct((m, D_MODEL), BF16)],
        scratch_shapes=[pltpu.VMEM((tm, D_MODEL), F32), pltpu.VMEM((tm, N_EXPERTS), F32)],
        compiler_params=_cparams(("arbitrary", "arbitrary", "arbitrary")),
        name="moe",
    )(x, xb, wr, wg, wu, wd, g, b)


def _s5_params(a_re, a_im, log_dt, b_re, b_im, c_re, c_im, d_skip, w_glu):
    g_n, p_n, c_n = N_SSM_GROUPS, SSM_STATE, SSM_GROUP
    dt = jnp.exp(log_dt)[:, None]
    mag = jnp.exp(a_re * dt)
    ab_re = mag * jnp.cos(a_im * dt)
    ab_im = mag * jnp.sin(a_im * dt)
    den = a_re * a_re + a_im * a_im
    nr = ab_re - 1.0
    f_re = (nr * a_re + ab_im * a_im) / den
    f_im = (ab_im * a_re - nr * a_im) / den
    bf_re = f_re[:, :, None] * b_re - f_im[:, :, None] * b_im
    bf_im = f_re[:, :, None] * b_im + f_im[:, :, None] * b_re
    eye = jnp.eye(g_n, dtype=F32)

    def blk_in(m):
        return jnp.einsum('gpc,gh->gchp', m, eye).reshape(g_n * c_n, g_n * p_n)

    def blk_out(m):
        return jnp.einsum('gcp,gh->gphc', m, eye).reshape(g_n * p_n, g_n * c_n)

    bblk = jnp.concatenate([blk_in(bf_re), blk_in(bf_im)], axis=1).astype(BF16)
    ccat = jnp.concatenate([blk_out(c_re), -blk_out(c_im)], axis=0).astype(BF16)
    ab = jnp.stack([ab_re.reshape(-1), ab_im.reshape(-1)])
    return bblk, ab, ccat, d_skip.reshape(1, D_SSM), w_glu.astype(BF16)


def _pad_cols(w, width):
    return jnp.pad(w, ((0, 0), (0, width - w.shape[1])))


def kernel(x_prompt, x_sample, cache_k, cache_v, cache_idx_k, state_ssm_re, state_ssm_im, state_conv, page_table,
           w_in, ssm_a_re, ssm_a_im, ssm_log_dt, ssm_b_re, ssm_b_im, ssm_c_re, ssm_c_im, ssm_d, ssm_w_glu,
           conv_w_dw, conv_b_dw, conv_ln_g, conv_ln_b, conv_w_pw, attn_w_o, w_out, ln1_g, ln1_b, ln2_g, ln2_b,
           ffn_w_gate, ffn_w_up, ffn_w_down, moe_w_router, moe_w_gate, moe_w_up, moe_w_down):
    n_b, t_p, _ = x_prompt.shape
    n_s, t_s, _ = x_sample.shape
    m_p, m_s = n_b * t_p, n_s * t_s
    n_pool = cache_k.shape[1]
    tq, tk = 256, 512
    nk = t_p // tk
    tm_p, tm_s = 512, 512
    t_tile = 512
    tf = D_FF // 2

    xp = x_prompt.reshape(m_p, D_MODEL)
    xs = x_sample.reshape(m_s, D_MODEL)
    xpb = xp.astype(BF16)
    xsb = xs.astype(BF16)
    pool_k = cache_k.reshape(DEPTH, n_pool, PAGE_SIZE, D_ATTN)
    pool_v = cache_v.reshape(DEPTH, n_pool, PAGE_SIZE, D_ATTN)
    row1 = lambda a: a.reshape(1, -1)

    outs = [[] for _ in range(12)]
    for l in range(DEPTH):
        w_a = w_in[l][:, C_U:C_KI].astype(BF16)
        w_b = jnp.concatenate([w_in[l][:, C_GL:C_END], _pad_cols(w_in[l][:, C_KI:C_WI], LANES),
                               _pad_cols(w_in[l][:, C_WI:C_GL], LANES)], axis=1).astype(BF16)
        s5w = _s5_params(ssm_a_re[l], ssm_a_im[l], ssm_log_dt[l], ssm_b_re[l], ssm_b_im[l], ssm_c_re[l],
                         ssm_c_im[l], ssm_d[l], ssm_w_glu[l])
        cw = (conv_w_dw[l], row1(conv_b_dw[l]), row1(conv_ln_g[l]), row1(conv_ln_b[l]), conv_w_pw[l].astype(BF16))
        wo = attn_w_o[l].astype(BF16)
        wout = w_out[l].astype(BF16)

        u, zc, qp, k, v, vb, qip = _in_proj_a(xpb, w_a, pad_heads=True, tm=tm_p)
        gl, ki, wi = _in_proj_b(xpb, w_b, tm=tm_p)
        ya, hfin = _s5_prompt(u, s5w, n_batch=n_b, t=t_p, tt=t_tile)
        yb, nbuf = _conv_prompt(zc, cw, n_batch=n_b, t=t_p, tt=t_tile)
        kt = k.astype(BF16).reshape(n_b, nk, tk, D_ATTN).transpose(0, 1, 3, 2)
        kit = ki.astype(BF16).reshape(n_b, nk, tk, IDX_DIM).transpose(0, 1, 3, 2)
        kit = jnp.concatenate([kit, kit], axis=2)
        at = _attn_prompt(qp, qip, wi, kt, vb.reshape(n_b, nk, tk, D_ATTN), kit, n_batch=n_b, t=t_p, tq=tq, tk=tk)
        xp1, xp1b = _merge(xp, ya, yb, at, gl, wo, wout, row1(ln1_g[l]), row1(ln1_b[l]), tm=tm_p)

        us, zcs, qs, ks, vs, _, qis = _in_proj_a(xsb, w_a, pad_heads=False, tm=tm_s)
        gls, kis, wis = _in_proj_b(xsb, w_b, tm=tm_s)
        seq3 = lambda a: a.reshape(n_s, t_s, a.shape[-1])
        tmajor = lambda a: seq3(a).transpose(1, 0, 2)
        nmajor = lambda a: a.transpose(1, 0, 2).reshape(m_s, a.shape[-1])
        yas, hsr, hsi = _s5_sample(tmajor(us), state_ssm_re[l].reshape(n_s, N_STATE),
                                   state_ssm_im[l].reshape(n_s, N_STATE), s5w, n=n_s, t=t_s)
        ybs, nbufs = _conv_sample(tmajor(zcs), state_conv[l].transpose(1, 0, 2), cw, n=n_s, t=t_s)
        yas, ybs = nmajor(yas), nmajor(ybs)
        ats = _attn_sample(page_table, seq3(qs), seq3(qis), seq3(wis), seq3(ks), seq3(vs), seq3(kis),
                           cache_idx_k, pool_k, pool_v, l, n=n_s, t=t_s)
        xs1, xs1b = _merge(xs, yas, ybs, ats.reshape(m_s, D_ATTN), gls, wo, wout, row1(ln1_g[l]), row1(ln1_b[l]),
                           tm=tm_s)

        j = l // 2
        g2, b2 = row1(ln2_g[l]), row1(ln2_b[l])
        if l % 2 == 0:
            wg, wu, wd = ffn_w_gate[j].astype(BF16), ffn_w_up[j].astype(BF16), ffn_w_down[j].astype(BF16)
            xp, xpb = _ffn(xp1, xp1b, wg, wu, wd, g2, b2, tm=tm_p, tf=tf)
            xs, xsb = _ffn(xs1, xs1b, wg, wu, wd, g2, b2, tm=tm_s, tf=tf)
        else:
            wg, wu, wd = moe_w_gate[j].astype(BF16), moe_w_up[j].astype(BF16), moe_w_down[j].astype(BF16)
            xp, xpb = _moe(xp1, xp1b, moe_w_router[j], wg, wu, wd, g2, b2, tm=tm_p, tf=tf)
            xs, xsb = _moe(xs1, xs1b, moe_w_router[j], wg, wu, wd, g2, b2, tm=tm_s, tf=tf)

        nb = CONV_WIDTH - 1
        layer_out = (
            k.reshape(n_b, t_p, N_HEADS, HEAD_DIM), v.reshape(n_b, t_p, N_HEADS, HEAD_DIM),
            ki.reshape(n_b, t_p, IDX_DIM),
            hfin[:, 0].reshape(n_b, N_SSM_GROUPS, SSM_STATE), hfin[:, 1].reshape(n_b, N_SSM_GROUPS, SSM_STATE),
            nbuf[:, CONV_HALO - nb:, :],
            ks.reshape(n_s, t_s, N_HEADS, HEAD_DIM), vs.reshape(n_s, t_s, N_HEADS, HEAD_DIM),
            kis.reshape(n_s, t_s, IDX_DIM),
            hsr.reshape(n_s, N_SSM_GROUPS, SSM_STATE), hsi.reshape(n_s, N_SSM_GROUPS, SSM_STATE),
            nbufs.transpose(1, 0, 2))
        for dst, val in zip(outs, layer_out):
            dst.append(val)

    return (xp.reshape(n_b, t_p, D_MODEL), xs.reshape(n_s, t_s, D_MODEL), *[jnp.stack(o) for o in outs])
```

```python
import functools

import jax
import jax.numpy as jnp
import numpy as np
from jax import lax
from jax.experimental import pallas as pl
from jax.experimental.pallas import tpu as pltpu

F32 = jnp.float32
BF16 = jnp.bfloat16
I32 = jnp.int32
I16 = jnp.int16

D_MODEL = 1024
DEPTH = 4
PAGE_SIZE = 128
D_SSM = 512
SSM_GROUP = 16
N_SSM_GROUPS = 32
SSM_STATE = 64
N_STATE = N_SSM_GROUPS * SSM_STATE
D_CONV = 512
CONV_WIDTH = 31
N_HEADS = 8
HEAD_DIM = 64
D_ATTN = 512
N_IDX_HEADS = 8
IDX_DIM = 64
TOPK_MAX = 256
ATTN_SCALE = HEAD_DIM ** -0.5
D_FF = 2816
N_EXPERTS = 8
DN_ALPHA = (2.0 * DEPTH) ** 0.25
LN_EPS = 1e-5
C_U, C_ZC, C_Q, C_K, C_V, C_QI, C_KI, C_WI, C_GL, C_END = 0, 512, 1536, 2048, 2560, 3072, 3584, 3648, 3656, 6728

LANES = 128
SUBLANES = 8
VMEM_LIMIT_BYTES = 56 * 1024 * 1024

INT_MIN = -(2 ** 31)
INT_MAX = 2 ** 31 - 1
I16_MIN = -(2 ** 15)
I16_MAX = 2 ** 15 - 1
NEG_BIG = -1e30


def _cparams(semantics=None):
    return pltpu.CompilerParams(dimension_semantics=semantics, vmem_limit_bytes=VMEM_LIMIT_BYTES)


def _layer_norm(x, g, b):
    mu = jnp.mean(x, axis=-1, keepdims=True)
    xc = x - mu
    var = jnp.mean(xc * xc, axis=-1, keepdims=True)
    return xc * lax.rsqrt(var + LN_EPS) * g + b


def _sortable_key(score):
    bits = lax.bitcast_convert_type(score, I32)
    return bits ^ ((bits >> 31) & 0x7FFFFFFF)


def _wrap_i32(v):
    return ((v + 2 ** 31) % 2 ** 32) - 2 ** 31


def _store_head_pairs(dst_ref, z, fill):
    lo = lax.broadcasted_iota(I32, (z.shape[0], LANES), 1) < HEAD_DIM
    for j in range(N_HEADS // 2):
        pair = z[:, LANES * j:LANES * (j + 1)]
        even = jnp.where(lo, pair, fill).astype(BF16)
        odd = jnp.where(lo, fill, pair).astype(BF16)
        if len(dst_ref.shape) == 3:
            dst_ref[2 * j] = even
            dst_ref[2 * j + 1] = odd
        else:
            dst_ref[:, 2 * LANES * j:2 * LANES * j + LANES] = even
            dst_ref[:, 2 * LANES * j + LANES:2 * LANES * (j + 1)] = odd


def _in_proj_a_body(x_ref, w_ref, u_ref, zc_ref, q_ref, k_ref, v_ref, vb_ref, qi_ref, *, pad_heads):
    x = x_ref[...]

    def mm(c0, c1):
        return jnp.dot(x, w_ref[:, c0:c1], preferred_element_type=F32)

    u_ref[...] = mm(C_U, C_ZC)
    zc_ref[...] = mm(C_ZC, C_Q)
    k_ref[...] = mm(C_K, C_V)
    v = mm(C_V, C_QI)
    v_ref[...] = v
    q = mm(C_Q, C_K) * ATTN_SCALE
    qi = mm(C_QI, C_KI)
    if pad_heads:
        _store_head_pairs(q_ref, q, 0.0)
        _store_head_pairs(qi_ref, qi, 0.0)
        _store_head_pairs(vb_ref, v, 1.0)
    else:
        q_ref[...] = q
        qi_ref[...] = qi
        vb_ref[...] = v.astype(BF16)


def _in_proj_a(xb, w_a, *, pad_heads, tm):
    m = xb.shape[0]
    row = lambda i: (i, 0)
    if pad_heads:
        qshape = jax.ShapeDtypeStruct((N_HEADS, m, LANES), BF16)
        qspec = pl.BlockSpec((N_HEADS, tm, LANES), lambda i: (0, i, 0))
    else:
        qshape = jax.ShapeDtypeStruct((m, D_ATTN), F32)
        qspec = pl.BlockSpec((tm, D_ATTN), row)
    vw = 2 * D_ATTN if pad_heads else D_ATTN
    return pl.pallas_call(
        functools.partial(_in_proj_a_body, pad_heads=pad_heads),
        grid=(m // tm,),
        in_specs=[pl.BlockSpec((tm, D_MODEL), row), pl.BlockSpec((D_MODEL, C_KI), lambda i: (0, 0))],
        out_specs=[pl.BlockSpec((tm, D_SSM), row), pl.BlockSpec((tm, 2 * D_CONV), row), qspec,
                   pl.BlockSpec((tm, D_ATTN), row), pl.BlockSpec((tm, D_ATTN), row), pl.BlockSpec((tm, vw), row),
                   qspec],
        out_shape=[jax.ShapeDtypeStruct((m, D_SSM), F32), jax.ShapeDtypeStruct((m, 2 * D_CONV), F32), qshape,
                   jax.ShapeDtypeStruct((m, D_ATTN), F32), jax.ShapeDtypeStruct((m, D_ATTN), F32),
                   jax.ShapeDtypeStruct((m, vw), BF16), qshape],
        compiler_params=_cparams(("arbitrary",)),
        name="in_proj_a",
    )(xb, w_a)


def _in_proj_b_body(x_ref, w_ref, gl_ref, ki_ref, wi_ref):
    x = x_ref[...]
    n_gl = 3 * D_MODEL
    gl_ref[...] = jnp.dot(x, w_ref[:, 0:n_gl], preferred_element_type=F32)
    ki_ref[...] = jnp.dot(x, w_ref[:, n_gl:n_gl + LANES], preferred_element_type=F32)[:, :IDX_DIM]
    wi_ref[...] = jnp.dot(x, w_ref[:, n_gl + LANES:n_gl + 2 * LANES], preferred_element_type=F32)[:, :N_IDX_HEADS]


def _in_proj_b(xb, w_b, *, tm):
    m = xb.shape[0]
    row = lambda i: (i, 0)
    n_gl = 3 * D_MODEL
    return pl.pallas_call(
        _in_proj_b_body,
        grid=(m // tm,),
        in_specs=[pl.BlockSpec((tm, D_MODEL), row), pl.BlockSpec((D_MODEL, n_gl + 2 * LANES), lambda i: (0, 0))],
        out_specs=[pl.BlockSpec((tm, n_gl), row), pl.BlockSpec((tm, IDX_DIM), row), pl.BlockSpec((tm, N_IDX_HEADS), row)],
        out_shape=[jax.ShapeDtypeStruct((m, n_gl), F32), jax.ShapeDtypeStruct((m, IDX_DIM), F32),
                   jax.ShapeDtypeStruct((m, N_IDX_HEADS), F32)],
        compiler_params=_cparams(("arbitrary",)),
        name="in_proj_b",
    )(xb, w_b)


S5_SCAN_UNROLL = 8


S5_BLOCKS = 2


def _s5_project_in(ub, bblk_ref):
    cu, cs = D_SSM // S5_BLOCKS, N_STATE // S5_BLOCKS
    parts = [[], []]
    for k in range(S5_BLOCKS):
        for part in range(2):
            c0 = part * N_STATE + k * cs
            parts[part].append(jnp.dot(ub[:, k * cu:(k + 1) * cu], bblk_ref[k * cu:(k + 1) * cu, c0:c0 + cs],
                                       preferred_element_type=F32))
    return jnp.concatenate(parts[0] + parts[1], axis=1)


def _s5_readout(h_bf16_re, h_bf16_im, u, ccat_ref, d_ref, wglu_ref):
    cu, cs = D_SSM // S5_BLOCKS, N_STATE // S5_BLOCKS
    ys = []
    for k in range(S5_BLOCKS):
        rs = slice(k * cs, (k + 1) * cs)
        ri = slice(N_STATE + k * cs, N_STATE + (k + 1) * cs)
        cols = slice(k * cu, (k + 1) * cu)
        ys.append(jnp.dot(h_bf16_re[:, rs], ccat_ref[rs, cols], preferred_element_type=F32)
                  + jnp.dot(h_bf16_im[:, rs], ccat_ref[ri, cols], preferred_element_type=F32))
    y = jnp.concatenate(ys, axis=1) + d_ref[...] * u
    y = jax.nn.gelu(y)
    y2 = jnp.dot(y.astype(BF16), wglu_ref[...], preferred_element_type=F32)
    return y2[:, :D_MODEL] * jax.nn.sigmoid(y2[:, D_MODEL:])


def _s5_prompt_body(u_ref, bblk_ref, ab_ref, ccat_ref, d_ref, wglu_ref, ya_ref, hfin_ref, xs_sc, hc_sc, *, tt):
    ti = pl.program_id(1)
    ns = N_STATE
    ar = ab_ref[0:1, :]
    ai = ab_ref[1:2, :]

    @pl.when(ti == 0)
    def _():
        hc_sc[...] = jnp.zeros_like(hc_sc)

    u = u_ref[...]
    xs_sc[...] = _s5_project_in(u.astype(BF16), bblk_ref)

    def step(t, carry):
        hr, hi = carry
        row = pl.ds(t, 1)
        nr = ar * hr - ai * hi + xs_sc[row, 0:ns]
        ni = ar * hi + ai * hr + xs_sc[row, ns:2 * ns]
        xs_sc[row, 0:ns] = nr
        xs_sc[row, ns:2 * ns] = ni
        return nr, ni

    hr, hi = lax.fori_loop(0, tt, step, (hc_sc[0:1, :], hc_sc[1:2, :]), unroll=S5_SCAN_UNROLL)
    hc_sc[0:1, :] = hr
    hc_sc[1:2, :] = hi
    hfin_ref[...] = hc_sc[...]
    ya_ref[...] = _s5_readout(xs_sc[:, 0:ns].astype(BF16), xs_sc[:, ns:2 * ns].astype(BF16), u,
                              ccat_ref, d_ref, wglu_ref)


def _s5_prompt(u, s5w, *, n_batch, t, tt):
    bblk, ab, ccat, d, wglu = s5w
    nt = t // tt
    const = lambda b, i: (0, 0)
    return pl.pallas_call(
        functools.partial(_s5_prompt_body, tt=tt),
        grid=(n_batch, nt),
        in_specs=[pl.BlockSpec((tt, D_SSM), lambda b, i: (b * nt + i, 0)),
                  pl.BlockSpec(bblk.shape, const), pl.BlockSpec(ab.shape, const), pl.BlockSpec(ccat.shape, const),
                  pl.BlockSpec(d.shape, const), pl.BlockSpec(wglu.shape, const)],
        out_specs=[pl.BlockSpec((tt, D_MODEL), lambda b, i: (b * nt + i, 0)),
                   pl.BlockSpec((None, 2, N_STATE), lambda b, i: (b, 0, 0))],
        out_shape=[jax.ShapeDtypeStruct((n_batch * t, D_MODEL), F32), jax.ShapeDtypeStruct((n_batch, 2, N_STATE), F32)],
        scratch_shapes=[pltpu.VMEM((tt, 2 * N_STATE), F32), pltpu.VMEM((2, N_STATE), F32)],
        compiler_params=_cparams(("arbitrary", "arbitrary")),
        name="s5_prompt",
    )(u, bblk, ab, ccat, d, wglu)


def _s5_sample_body(u_ref, h0r_ref, h0i_ref, bblk_ref, ab_ref, ccat_ref, d_ref, wglu_ref, ya_ref, hr_ref, hi_ref,
                    *, n, t):
    ns = N_STATE
    ar = ab_ref[0:1, :]
    ai = ab_ref[1:2, :]
    hr = h0r_ref[...]
    hi = h0i_ref[...]
    for step in range(t):
        u = u_ref[step]
        x = _s5_project_in(u.astype(BF16), bblk_ref)
        hr, hi = ar * hr - ai * hi + x[:, 0:ns], ar * hi + ai * hr + x[:, ns:2 * ns]
        ya_ref[step] = _s5_readout(hr.astype(BF16), hi.astype(BF16), u, ccat_ref, d_ref, wglu_ref)
    hr_ref[...] = hr
    hi_ref[...] = hi


def _s5_sample(u, h0r, h0i, s5w, *, n, t):
    bblk, ab, ccat, d, wglu = s5w
    return pl.pallas_call(
        functools.partial(_s5_sample_body, n=n, t=t),
        out_shape=[jax.ShapeDtypeStruct((t, n, D_MODEL), F32), jax.ShapeDtypeStruct((n, N_STATE), F32),
                   jax.ShapeDtypeStruct((n, N_STATE), F32)],
        compiler_params=_cparams(),
        name="s5_sample",
    )(u, h0r, h0i, bblk, ab, ccat, d, wglu)


CONV_HALO = 32
CONV_ROWS = 64


def _conv_tail(y, bdw_ref, lng_ref, lnb_ref, wpw_ref):
    y = _layer_norm(y + bdw_ref[...], lng_ref[...], lnb_ref[...])
    y = y * jax.nn.sigmoid(y)
    return jnp.dot(y.astype(BF16), wpw_ref[...], preferred_element_type=F32)


def _conv_prompt_body(zc_ref, wdw_ref, bdw_ref, lng_ref, lnb_ref, wpw_ref, yb_ref, nb_ref, vb_sc, y_sc, *, tt):
    ti = pl.program_id(1)
    halo = CONV_HALO
    off = halo - (CONV_WIDTH - 1)

    @pl.when(ti == 0)
    def _():
        vb_sc[0:halo, :] = jnp.zeros((halo, D_CONV), F32)

    zc = zc_ref[...]
    vb_sc[halo:halo + tt, :] = zc[:, :D_CONV] * jax.nn.sigmoid(zc[:, D_CONV:])
    for r0 in range(0, tt, CONV_ROWS):
        acc = jnp.zeros((CONV_ROWS, D_CONV), F32)
        for j in range(CONV_WIDTH):
            acc = acc + wdw_ref[j:j + 1, :] * vb_sc[r0 + j + off:r0 + j + off + CONV_ROWS, :]
        y_sc[r0:r0 + CONV_ROWS, :] = acc
    yb_ref[...] = _conv_tail(y_sc[...], bdw_ref, lng_ref, lnb_ref, wpw_ref)
    tail = vb_sc[tt:tt + halo, :]
    nb_ref[...] = tail
    vb_sc[0:halo, :] = tail


def _conv_prompt(zc, cw, *, n_batch, t, tt):
    wdw, bdw, lng, lnb, wpw = cw
    nt = t // tt
    const = lambda b, i: (0, 0)
    return pl.pallas_call(
        functools.partial(_conv_prompt_body, tt=tt),
        grid=(n_batch, nt),
        in_specs=[pl.BlockSpec((tt, 2 * D_CONV), lambda b, i: (b * nt + i, 0)),
                  pl.BlockSpec(wdw.shape, const), pl.BlockSpec(bdw.shape, const), pl.BlockSpec(lng.shape, const),
                  pl.BlockSpec(lnb.shape, const), pl.BlockSpec(wpw.shape, const)],
        out_specs=[pl.BlockSpec((tt, D_MODEL), lambda b, i: (b * nt + i, 0)),
                   pl.BlockSpec((None, CONV_HALO, D_CONV), lambda b, i: (b, 0, 0))],
        out_shape=[jax.ShapeDtypeStruct((n_batch * t, D_MODEL), F32),
                   jax.ShapeDtypeStruct((n_batch, CONV_HALO, D_CONV), F32)],
        scratch_shapes=[pltpu.VMEM((tt + CONV_HALO, D_CONV), F32), pltpu.VMEM((tt, D_CONV), F32)],
        compiler_params=_cparams(("arbitrary", "arbitrary")),
        name="conv_prompt",
    )(zc, wdw, bdw, lng, lnb, wpw)


def _conv_sample_body(zc_ref, buf_ref, wdw_ref, bdw_ref, lng_ref, lnb_ref, wpw_ref, yb_ref, nb_ref, vp_sc, *, n, t):
    nbuf = CONV_WIDTH - 1
    vp_sc[0:nbuf] = buf_ref[...]
    for step in range(t):
        zc = zc_ref[step]
        vp_sc[nbuf + step] = zc[:, :D_CONV] * jax.nn.sigmoid(zc[:, D_CONV:])
    half = D_CONV // 2
    for step in range(t):
        cols = []
        for c0 in range(0, D_CONV, half):
            acc = jnp.zeros((n, half), F32)
            for j in range(CONV_WIDTH):
                acc = acc + wdw_ref[j:j + 1, c0:c0 + half] * vp_sc[step + j, :, c0:c0 + half]
            cols.append(acc)
        y = jnp.concatenate(cols, axis=1)
        yb_ref[step] = _conv_tail(y, bdw_ref, lng_ref, lnb_ref, wpw_ref)
    nb_ref[...] = vp_sc[t:t + nbuf]


def _conv_sample(zc, buf, cw, *, n, t):
    wdw, bdw, lng, lnb, wpw = cw
    nbuf = CONV_WIDTH - 1
    return pl.pallas_call(
        functools.partial(_conv_sample_body, n=n, t=t),
        out_shape=[jax.ShapeDtypeStruct((t, n, D_MODEL), F32), jax.ShapeDtypeStruct((nbuf, n, D_CONV), F32)],
        scratch_shapes=[pltpu.VMEM((nbuf + t, n, D_CONV), F32)],
        compiler_params=_cparams(),
        name="conv_sample",
    )(zc, buf, wdw, bdw, lng, lnb, wpw)


def _attn_prompt_body(qp_ref, qip_ref, wi_ref, kt_ref, v_ref, kit_ref, o_ref, keys_sc, hi_sc, lo_sc, tsel_sc, m_sc,
                      acc_sc,
                      *, tq, tk, n_kblk, k_sel):
    i = pl.program_id(1)
    n_ch = (i * tq + tq - 1) // tk + 1
    row = lax.broadcasted_iota(I32, (tq, tk), 0)
    lane = lax.broadcasted_iota(I32, (tq, tk), 1)
    qpos = i * tq + row
    w = wi_ref[...]

    qi_all = qip_ref[...].reshape(N_IDX_HEADS * tq, LANES)

    def score_chunk(c, _):
        d = jnp.dot(qi_all, kit_ref[c], preferred_element_type=F32)
        acc = jnp.zeros((tq, tk), F32)
        for h in range(N_IDX_HEADS):
            acc = acc + w[:, h:h + 1] * jnp.maximum(d[h * tq:(h + 1) * tq], 0.0)
        key = jnp.where(c * tk + lane <= qpos, _sortable_key(acc), INT_MIN)
        keys_sc[c] = key
        hi_sc[c] = (key >> 16).astype(I16)
        lo_sc[c] = ((key & 0xFFFF) + I16_MIN).astype(I16)
        return 0

    lax.fori_loop(0, n_ch, score_chunk, 0)
    n_lt = tk // LANES

    n_total = (n_ch * tk).astype(F32)

    def count16(src_sc, cand):
        cand16 = jnp.broadcast_to(cand, (tq, LANES)).astype(I16)

        def body(c, part):
            for s in range(n_lt):
                part = part + jnp.where(src_sc[c, :, LANES * s:LANES * (s + 1)] >= cand16, jnp.int16(1), jnp.int16(0))
            return part

        part = lax.fori_loop(0, n_ch, body, jnp.zeros((tq, LANES), I16))
        return jnp.sum(part.astype(I32).astype(F32), axis=1, keepdims=True)

    def search16(src_sc, want, count_at_min):
        def bit_body(b, carry):
            t16, acc = carry
            cand = t16 + lax.shift_left(jnp.int32(1), 15 - b)
            cnt = count16(src_sc, cand)
            ok = cnt >= want
            return jnp.where(ok, cand, t16), jnp.where(ok, cnt, acc)

        return lax.fori_loop(0, 16, bit_body, (jnp.full((tq, 1), I16_MIN, I32), count_at_min))

    ksel = jnp.full((tq, 1), float(k_sel), F32)
    t_hi, n_ge_hi = search16(hi_sc, ksel, jnp.zeros((tq, 1), F32) + n_total)
    n_gt_hi = jnp.where(t_hi == I16_MAX, 0.0, count16(hi_sc, jnp.minimum(t_hi + 1, I16_MAX)))
    want_lo = ksel - n_gt_hi
    t_hi16 = jnp.concatenate([jnp.broadcast_to(t_hi, (tq, LANES)).astype(I16)] * n_lt, axis=1)

    def keep_matching(c, _):
        lo_sc[c] = jnp.where(hi_sc[c] == t_hi16, lo_sc[c], jnp.int16(I16_MIN))
        return 0

    lax.fori_loop(0, n_ch, keep_matching, 0)
    t_lo, n_ge_lo = search16(lo_sc, want_lo, n_ge_hi - n_gt_hi)
    t = t_hi * 65536 + (t_lo - I16_MIN)
    tsel_sc[...] = jnp.maximum(t, INT_MIN + 1)

    tie = (n_ge_lo > want_lo) & (t > INT_MIN)

    @pl.when(jnp.max(jnp.where(tie, 1.0, 0.0)) > 0.0)
    def _():
        def count(pred):
            def body(c, part):
                hit = jnp.where(pred(c, keys_sc[c]), 1.0, 0.0)
                for s in range(n_lt):
                    part = part + hit[:, LANES * s:LANES * (s + 1)]
                return part

            part = lax.fori_loop(0, n_ch, body, jnp.zeros((tq, LANES), F32))
            return jnp.sum(part, axis=1, keepdims=True)

        need = ksel - count(lambda c, kk: kk > t)
        nbits = (n_kblk * tk - 1).bit_length()

        def col_body(b, p):
            cand = p + lax.shift_left(jnp.int32(1), nbits - 1 - b)
            cnt = count(lambda c, kk: (kk == t) & (c * tk + lane < cand))
            return jnp.where(cnt < need, cand, p)

        last = lax.fori_loop(0, nbits, col_body, jnp.zeros((tq, 1), I32))

        def drop(c, _):
            kk = keys_sc[c]
            keys_sc[c] = jnp.where(tie & (kk == t) & (c * tk + lane > last), INT_MIN, kk)
            return 0

        lax.fori_loop(0, n_ch, drop, 0)

    t_sel = tsel_sc[...]

    m_sc[...] = jnp.full(m_sc.shape, NEG_BIG, F32)
    acc_sc[...] = jnp.zeros(acc_sc.shape, F32)

    def attend_chunk(c, _):
        bias = jnp.where(keys_sc[c] >= t_sel, 0.0, NEG_BIG)
        for j in range(N_HEADS // 2):
            q_pair = qp_ref[2 * j:2 * j + 2].reshape(2 * tq, LANES)
            s_pair = jnp.dot(q_pair, kt_ref[c, LANES * j:LANES * (j + 1), :], preferred_element_type=F32)
            for h in (2 * j, 2 * j + 1):
                s = s_pair[(h % 2) * tq:(h % 2 + 1) * tq] + bias
                m_prev = m_sc[h]
                m_new = jnp.maximum(m_prev, jnp.max(s, axis=1, keepdims=True))
                m_sc[h] = m_new
                p = jnp.exp(s - jnp.concatenate([m_new] * n_lt, axis=1))
                pv = jnp.dot(p.astype(BF16), v_ref[c, :, LANES * h:LANES * (h + 1)], preferred_element_type=F32)
                acc_sc[h] = acc_sc[h] * jnp.exp(m_prev - m_new) + pv
        return 0

    lax.fori_loop(0, n_ch, attend_chunk, 0)
    lo = lax.broadcasted_iota(I32, (tq, LANES), 1) < HEAD_DIM
    for j in range(N_HEADS // 2):
        even = acc_sc[2 * j]
        odd = acc_sc[2 * j + 1]
        out = jnp.where(lo, even / pltpu.roll(even, HEAD_DIM, axis=1), odd / pltpu.roll(odd, HEAD_DIM, axis=1))
        o_ref[:, LANES * j:LANES * (j + 1)] = out.astype(BF16)


def _attn_prompt(qp, qip, wi, kt, vb, kit, *, n_batch, t, tq, tk):
    nq = t // tq
    nk = t // tk
    k_sel = min(TOPK_MAX, t // 4)
    qrow = lambda b, i: (b * nq + i, 0)
    whole = lambda b, i: (b, 0, 0, 0)
    return pl.pallas_call(
        functools.partial(_attn_prompt_body, tq=tq, tk=tk, n_kblk=nk, k_sel=k_sel),
        grid=(n_batch, nq),
        in_specs=[pl.BlockSpec((N_HEADS, tq, LANES), lambda b, i: (0, b * nq + i, 0)),
                  pl.BlockSpec((N_IDX_HEADS, tq, LANES), lambda b, i: (0, b * nq + i, 0)),
                  pl.BlockSpec((tq, N_IDX_HEADS), qrow),
                  pl.BlockSpec((None, nk, D_ATTN, tk), whole, pipeline_mode=pl.Buffered(1)),
                  pl.BlockSpec((None, nk, tk, 2 * D_ATTN), whole, pipeline_mode=pl.Buffered(1)),
                  pl.BlockSpec((None, nk, LANES, tk), whole, pipeline_mode=pl.Buffered(1))],
        out_specs=pl.BlockSpec((tq, D_ATTN), qrow),
        out_shape=jax.ShapeDtypeStruct((n_batch * t, D_ATTN), BF16),
        scratch_shapes=[pltpu.VMEM((nk, tq, tk), I32), pltpu.VMEM((nk, tq, tk), I16), pltpu.VMEM((nk, tq, tk), I16),
                        pltpu.VMEM((tq, 1), I32),
                        pltpu.VMEM((N_HEADS, tq, LANES), F32), pltpu.VMEM((N_HEADS, tq, LANES), F32)],
        compiler_params=_cparams(("arbitrary", "arbitrary")),
        name="attn_prompt",
    )(qp, qip, wi, kt, vb, kit)


SAMPLE_RADIX_BITS = 4


def _attn_sample_body(pt_ref, q_ref, qi_ref, wi_ref, kn_ref, vn_ref, kin_ref, *rest, n_pages, t, k_sel):
    del pt_ref
    ki_pages = rest[0:n_pages]
    k_pages = rest[n_pages:2 * n_pages]
    v_pages = rest[2 * n_pages:3 * n_pages]
    o_ref, kit_sc, kt_sc, vt_sc, keys_sc = rest[3 * n_pages:]
    past = n_pages * PAGE_SIZE
    nkp = past + PAGE_SIZE
    rows = N_HEADS * t

    for j in range(n_pages):
        sl = slice(PAGE_SIZE * j, PAGE_SIZE * (j + 1))
        kit_sc[:, sl] = ki_pages[j][...].astype(BF16)
        kt_sc[:, sl] = k_pages[j][...].astype(BF16)
        vt_sc[:, sl] = v_pages[j][...].astype(BF16)

    def tail_t(new):
        padded = jnp.concatenate([new, jnp.zeros((PAGE_SIZE - t, new.shape[1]), F32)], axis=0)
        return padded.T.astype(BF16)

    kin = kin_ref[...]
    kin_wide = jnp.concatenate([kin, jnp.zeros((t, LANES - IDX_DIM), F32)], axis=1)
    kit_sc[:, past:nkp] = tail_t(kin_wide)[0:IDX_DIM, :]
    kt_sc[:, past:nkp] = tail_t(kn_ref[...])
    vt_sc[:, past:nkp] = tail_t(vn_ref[...])

    r_i = lax.broadcasted_iota(I32, (rows, D_ATTN), 0)
    l_i = lax.broadcasted_iota(I32, (rows, D_ATTN), 1)
    own = (l_i // HEAD_DIM) == (r_i // t)
    qblk = jnp.where(own, jnp.concatenate([q_ref[...]] * N_HEADS, axis=0), 0.0).astype(BF16)
    qiblk = jnp.where(own, jnp.concatenate([qi_ref[...]] * N_IDX_HEADS, axis=0), 0.0).astype(BF16)
    fr = lax.broadcasted_iota(I32, (D_ATTN, IDX_DIM), 0)
    fc = lax.broadcasted_iota(I32, (D_ATTN, IDX_DIM), 1)
    fold = jnp.where((fr % IDX_DIM) == fc, 1.0, 0.0).astype(BF16)
    qi_rows = jnp.dot(qiblk, fold, preferred_element_type=F32).astype(BF16)

    dots = jnp.dot(qi_rows, kit_sc[...], preferred_element_type=F32)
    w = wi_ref[...]
    score = jnp.zeros((t, nkp), F32)
    for h in range(N_IDX_HEADS):
        score = score + w[:, h:h + 1] * jnp.maximum(dots[t * h:t * (h + 1), :], 0.0)
    col = lax.broadcasted_iota(I32, (t, nkp), 1)
    qpos = past + lax.broadcasted_iota(I32, (t, nkp), 0)
    keys = jnp.where(col <= qpos, _sortable_key(score), INT_MIN)
    keys_sc[...] = keys

    def count(mask):
        return jnp.sum(jnp.where(mask, 1.0, 0.0), axis=1, keepdims=True)

    rb = SAMPLE_RADIX_BITS
    thr = jnp.full((t, 1), INT_MIN, I32)
    for p in range(32 // rb):
        shift = 32 - rb * (p + 1)
        inc = jnp.zeros((t, 1), I32)
        for jj in range(1, 2 ** rb):
            cand = thr + jnp.int32(_wrap_i32(jj << shift))
            inc = inc + jnp.where(count(keys >= cand) >= k_sel, 1, 0).astype(I32)
        thr = thr + lax.shift_left(inc, jnp.int32(shift))

    cnt_gt = count(keys >= thr + 1)
    need = k_sel - cnt_gt
    tie = ((count(keys >= thr) - cnt_gt) > need) & (thr > INT_MIN)

    @pl.when(jnp.max(jnp.where(tie, 1.0, 0.0)) > 0.0)
    def _():
        nbits = (nkp - 1).bit_length()
        last = jnp.zeros((t, 1), I32)
        for b in range(nbits):
            cand = last + (1 << (nbits - 1 - b))
            last = jnp.where(count((keys == thr) & (col < cand)) < need, cand, last)
        keys_sc[...] = jnp.where(tie & (keys == thr) & (col > last), INT_MIN, keys)

    bias = jnp.where(keys_sc[...] >= jnp.maximum(thr, INT_MIN + 1), 0.0, NEG_BIG)
    s = jnp.dot(qblk, kt_sc[...], preferred_element_type=F32) + jnp.concatenate([bias] * N_HEADS, axis=0)
    m = jnp.max(s, axis=1, keepdims=True)
    p = jnp.exp(s - m)
    inv_l = 1.0 / jnp.sum(p, axis=1, keepdims=True)
    nt = (((1,), (1,)), ((), ()))
    o_full = lax.dot_general(p.astype(BF16), vt_sc[...], nt, preferred_element_type=F32) * inv_l
    l_o = lax.broadcasted_iota(I32, (t, D_ATTN), 1)
    out = jnp.zeros((t, D_ATTN), F32)
    for h in range(N_HEADS):
        out = out + jnp.where((l_o // HEAD_DIM) == h, o_full[t * h:t * (h + 1), :], 0.0)
    o_ref[...] = out.astype(BF16)


def _attn_sample(page_table, q, qi, wi, kn, vn, kin, pool_ki, pool_k, pool_v, layer, *, n, t):
    n_pages = page_table.shape[1]
    past = n_pages * PAGE_SIZE
    nkp = past + PAGE_SIZE
    k_sel = min(TOPK_MAX, (past + t) // 4)
    seq = lambda i, pt: (i, 0, 0)

    def page_spec(c, j):
        return pl.BlockSpec((None, None, c, PAGE_SIZE), lambda i, pt, j=j: (layer, pt[i, j], 0, 0))

    in_specs = [pl.BlockSpec((None, t, D_ATTN), seq), pl.BlockSpec((None, t, D_ATTN), seq),
                pl.BlockSpec((None, t, N_IDX_HEADS), seq), pl.BlockSpec((None, t, D_ATTN), seq),
                pl.BlockSpec((None, t, D_ATTN), seq), pl.BlockSpec((None, t, IDX_DIM), seq)]
    in_specs += [page_spec(IDX_DIM, j) for j in range(n_pages)]
    in_specs += [page_spec(D_ATTN, j) for j in range(n_pages)]
    in_specs += [page_spec(D_ATTN, j) for j in range(n_pages)]
    grid_spec = pltpu.PrefetchScalarGridSpec(
        num_scalar_prefetch=1, grid=(n,), in_specs=in_specs,
        out_specs=pl.BlockSpec((None, t, D_ATTN), seq),
        scratch_shapes=[pltpu.VMEM((IDX_DIM, nkp), BF16), pltpu.VMEM((D_ATTN, nkp), BF16),
                        pltpu.VMEM((D_ATTN, nkp), BF16), pltpu.VMEM((t, nkp), I32)])
    return pl.pallas_call(
        functools.partial(_attn_sample_body, n_pages=n_pages, t=t, k_sel=k_sel),
        grid_spec=grid_spec,
        out_shape=jax.ShapeDtypeStruct((n, t, D_ATTN), BF16),
        compiler_params=_cparams(("arbitrary",)),
        name="attn_sample",
    )(page_table, q, qi, wi, kn, vn, kin, *([pool_ki] * n_pages), *([pool_k] * n_pages), *([pool_v] * n_pages))


def _merge_body(x_ref, ya_ref, yb_ref, at_ref, gl_ref, wo_ref, wout_ref, g_ref, b_ref, xo_ref, xob_ref):
    yc = jnp.dot(at_ref[...], wo_ref[...], preferred_element_type=F32)
    gl = gl_ref[...]
    merged = (jax.nn.sigmoid(gl[:, 0:D_MODEL]) * ya_ref[...]
              + jax.nn.sigmoid(gl[:, D_MODEL:2 * D_MODEL]) * yb_ref[...]
              + jax.nn.sigmoid(gl[:, 2 * D_MODEL:3 * D_MODEL]) * yc)
    mix = jnp.dot(merged.astype(BF16), wout_ref[...], preferred_element_type=F32)
    xn = _layer_norm(DN_ALPHA * x_ref[...] + mix, g_ref[...], b_ref[...])
    xo_ref[...] = xn
    xob_ref[...] = xn.astype(BF16)


def _merge(x, ya, yb, at, gl, wo, wout, g, b, *, tm):
    m = x.shape[0]
    row = lambda i: (i, 0)
    const = lambda i: (0, 0)
    return pl.pallas_call(
        _merge_body,
        grid=(m // tm,),
        in_specs=[pl.BlockSpec((tm, D_MODEL), row), pl.BlockSpec((tm, D_MODEL), row), pl.BlockSpec((tm, D_MODEL), row),
                  pl.BlockSpec((tm, D_ATTN), row), pl.BlockSpec((tm, 3 * D_MODEL), row),
                  pl.BlockSpec(wo.shape, const), pl.BlockSpec(wout.shape, const),
                  pl.BlockSpec(g.shape, const), pl.BlockSpec(b.shape, const)],
        out_specs=[pl.BlockSpec((tm, D_MODEL), row), pl.BlockSpec((tm, D_MODEL), row)],
        out_shape=[jax.ShapeDtypeStruct((m, D_MODEL), F32), jax.ShapeDtypeStruct((m, D_MODEL), BF16)],
        compiler_params=_cparams(("arbitrary",)),
        name="merge",
    )(x, ya, yb, at, gl, wo, wout, g, b)


def _swiglu_part(xb, wg, wu, wd):
    g = jnp.dot(xb, wg, preferred_element_type=F32)
    h = (g * jax.nn.sigmoid(g)) * jnp.dot(xb, wu, preferred_element_type=F32)
    return jnp.dot(h.astype(BF16), wd, preferred_element_type=F32)


def _finish_layer(x_ref, acc, g_ref, b_ref, xo_ref, xob_ref):
    xn = _layer_norm(DN_ALPHA * x_ref[...] + acc, g_ref[...], b_ref[...])
    xo_ref[...] = xn
    xob_ref[...] = xn.astype(BF16)


def _ffn_body(x_ref, xb_ref, wg_ref, wu_ref, wd_ref, g_ref, b_ref, xo_ref, xob_ref, acc_sc):
    f = pl.program_id(1)
    part = _swiglu_part(xb_ref[...], wg_ref[...], wu_ref[...], wd_ref[...])

    @pl.when(f == 0)
    def _():
        acc_sc[...] = part

    @pl.when(f > 0)
    def _():
        acc_sc[...] = acc_sc[...] + part

    @pl.when(f == pl.num_programs(1) - 1)
    def _():
        _finish_layer(x_ref, acc_sc[...], g_ref, b_ref, xo_ref, xob_ref)


def _ffn(x, xb, wg, wu, wd, g, b, *, tm, tf):
    m = x.shape[0]
    row = lambda i, f: (i, 0)
    const = lambda i, f: (0, 0)
    return pl.pallas_call(
        _ffn_body,
        grid=(m // tm, D_FF // tf),
        in_specs=[pl.BlockSpec((tm, D_MODEL), row), pl.BlockSpec((tm, D_MODEL), row),
                  pl.BlockSpec((D_MODEL, tf), lambda i, f: (0, f)), pl.BlockSpec((D_MODEL, tf), lambda i, f: (0, f)),
                  pl.BlockSpec((tf, D_MODEL), lambda i, f: (f, 0)),
                  pl.BlockSpec(g.shape, const), pl.BlockSpec(b.shape, const)],
        out_specs=[pl.BlockSpec((tm, D_MODEL), row), pl.BlockSpec((tm, D_MODEL), row)],
        out_shape=[jax.ShapeDtypeStruct((m, D_MODEL), F32), jax.ShapeDtypeStruct((m, D_MODEL), BF16)],
        scratch_shapes=[pltpu.VMEM((tm, D_MODEL), F32)],
        compiler_params=_cparams(("arbitrary", "arbitrary")),
        name="ffn",
    )(x, xb, wg, wu, wd, g, b)


def _moe_body(x_ref, xb_ref, wr_ref, wg_ref, wu_ref, wd_ref, g_ref, b_ref, xo_ref, xob_ref, acc_sc, gate_sc):
    e = pl.program_id(1)
    f = pl.program_id(2)
    tm = x_ref.shape[0]
    ids = lax.broadcasted_iota(I32, (tm, N_EXPERTS), 1).astype(F32)

    @pl.when((e == 0) & (f == 0))
    def _():
        logits = jnp.dot(x_ref[...], wr_ref[...], preferred_element_type=F32, precision=lax.Precision.HIGHEST)
        m1 = jnp.max(logits, axis=1, keepdims=True)
        i1 = jnp.min(jnp.where(logits == m1, ids, float(N_EXPERTS)), axis=1, keepdims=True)
        rest = jnp.where(ids == i1, -jnp.inf, logits)
        m2 = jnp.max(rest, axis=1, keepdims=True)
        i2 = jnp.min(jnp.where(rest == m2, ids, float(N_EXPERTS)), axis=1, keepdims=True)
        e2 = jnp.exp(m2 - m1)
        p1 = 1.0 / (1.0 + e2)
        gate_sc[...] = jnp.where(ids == i1, p1, 0.0) + jnp.where(ids == i2, e2 * p1, 0.0)
        acc_sc[...] = jnp.zeros_like(acc_sc)

    gate = jnp.sum(jnp.where(ids == e.astype(F32), gate_sc[...], 0.0), axis=1, keepdims=True)
    acc_sc[...] = acc_sc[...] + gate * _swiglu_part(xb_ref[...], wg_ref[...], wu_ref[...], wd_ref[...])

    @pl.when((e == pl.num_programs(1) - 1) & (f == pl.num_programs(2) - 1))
    def _():
        _finish_layer(x_ref, acc_sc[...], g_ref, b_ref, xo_ref, xob_ref)


def _moe(x, xb, wr, wg, wu, wd, g, b, *, tm, tf):
    m = x.shape[0]
    row = lambda i, e, f: (i, 0)
    const = lambda i, e, f: (0, 0)
    return pl.pallas_call(
        _moe_body,
        grid=(m // tm, N_EXPERTS, D_FF // tf),
        in_specs=[pl.BlockSpec((tm, D_MODEL), row), pl.BlockSpec((tm, D_MODEL), row),
                  pl.BlockSpec(wr.shape, const),
                  pl.BlockSpec((None, D_MODEL, tf), lambda i, e, f: (e, 0, f)),
                  pl.BlockSpec((None, D_MODEL, tf), lambda i, e, f: (e, 0, f)),
                  pl.BlockSpec((None, tf, D_MODEL), lambda i, e, f: (e, f, 0)),
                  pl.BlockSpec(g.shape, const), pl.BlockSpec(b.shape, const)],
        out_specs=[pl.BlockSpec((tm, D_MODEL), row), pl.BlockSpec((tm, D_MODEL), row)],
        out_shape=[jax.ShapeDtypeStruct((m, D_MODEL), F32), jax.ShapeDtypeStruct((m, D_MODEL), BF16)],
        scratch_shapes=[pltpu.VMEM((tm, D_MODEL), F32), pltpu.VMEM((tm, N_EXPERTS), F32)],
        compiler_params=_cparams(("arbitrary", "arbitrary", "arbitrary")),
        name="moe",
    )(x, xb, wr, wg, wu, wd, g, b)


def _s5_params(a_re, a_im, log_dt, b_re, b_im, c_re, c_im, d_skip, w_glu):
    g_n, p_n, c_n = N_SSM_GROUPS, SSM_STATE, SSM_GROUP
    dt = jnp.exp(log_dt)[:, None]
    mag = jnp.exp(a_re * dt)
    ab_re = mag * jnp.cos(a_im * dt)
    ab_im = mag * jnp.sin(a_im * dt)
    den = a_re * a_re + a_im * a_im
    nr = ab_re - 1.0
    f_re = (nr * a_re + ab_im * a_im) / den
    f_im = (ab_im * a_re - nr * a_im) / den
    bf_re = f_re[:, :, None] * b_re - f_im[:, :, None] * b_im
    bf_im = f_re[:, :, None] * b_im + f_im[:, :, None] * b_re
    eye = jnp.eye(g_n, dtype=F32)

    def blk_in(m):
        return jnp.einsum('gpc,gh->gchp', m, eye).reshape(g_n * c_n, g_n * p_n)

    def blk_out(m):
        return jnp.einsum('gcp,gh->gphc', m, eye).reshape(g_n * p_n, g_n * c_n)

    bblk = jnp.concatenate([blk_in(bf_re), blk_in(bf_im)], axis=1).astype(BF16)
    ccat = jnp.concatenate([blk_out(c_re), -blk_out(c_im)], axis=0).astype(BF16)
    ab = jnp.stack([ab_re.reshape(-1), ab_im.reshape(-1)])
    return bblk, ab, ccat, d_skip.reshape(1, D_SSM), w_glu.astype(BF16)


def _pad_cols(w, width):
    return jnp.pad(w, ((0, 0), (0, width - w.shape[1])))


def kernel(x_prompt, x_sample, cache_k, cache_v, cache_idx_k, state_ssm_re, state_ssm_im, state_conv, page_table,
           w_in, ssm_a_re, ssm_a_im, ssm_log_dt, ssm_b_re, ssm_b_im, ssm_c_re, ssm_c_im, ssm_d, ssm_w_glu,
           conv_w_dw, conv_b_dw, conv_ln_g, conv_ln_b, conv_w_pw, attn_w_o, w_out, ln1_g, ln1_b, ln2_g, ln2_b,
           ffn_w_gate, ffn_w_up, ffn_w_down, moe_w_router, moe_w_gate, moe_w_up, moe_w_down):
    n_b, t_p, _ = x_prompt.shape
    n_s, t_s, _ = x_sample.shape
    m_p, m_s = n_b * t_p, n_s * t_s
    n_pool = cache_k.shape[1]
    tq, tk = 256, 512
    nk = t_p // tk
    tm_p, tm_s = 512, 512
    t_tile = 512
    tf = D_FF // 2

    xp = x_prompt.reshape(m_p, D_MODEL)
    xs = x_sample.reshape(m_s, D_MODEL)
    xpb = xp.astype(BF16)
    xsb = xs.astype(BF16)
    pool_k = cache_k.transpose(0, 1, 3, 4, 2).reshape(DEPTH, n_pool, D_ATTN, PAGE_SIZE)
    pool_v = cache_v.transpose(0, 1, 3, 4, 2).reshape(DEPTH, n_pool, D_ATTN, PAGE_SIZE)
    pool_ki = cache_idx_k.transpose(0, 1, 3, 2)
    row1 = lambda a: a.reshape(1, -1)

    outs = [[] for _ in range(12)]
    for l in range(DEPTH):
        w_a = w_in[l][:, C_U:C_KI].astype(BF16)
        w_b = jnp.concatenate([w_in[l][:, C_GL:C_END], _pad_cols(w_in[l][:, C_KI:C_WI], LANES),
                               _pad_cols(w_in[l][:, C_WI:C_GL], LANES)], axis=1).astype(BF16)
        s5w = _s5_params(ssm_a_re[l], ssm_a_im[l], ssm_log_dt[l], ssm_b_re[l], ssm_b_im[l], ssm_c_re[l],
                         ssm_c_im[l], ssm_d[l], ssm_w_glu[l])
        cw = (conv_w_dw[l], row1(conv_b_dw[l]), row1(conv_ln_g[l]), row1(conv_ln_b[l]), conv_w_pw[l].astype(BF16))
        wo = attn_w_o[l].astype(BF16)
        wout = w_out[l].astype(BF16)

        u, zc, qp, k, v, vb, qip = _in_proj_a(xpb, w_a, pad_heads=True, tm=tm_p)
        gl, ki, wi = _in_proj_b(xpb, w_b, tm=tm_p)
        ya, hfin = _s5_prompt(u, s5w, n_batch=n_b, t=t_p, tt=t_tile)
        yb, nbuf = _conv_prompt(zc, cw, n_batch=n_b, t=t_p, tt=t_tile)
        kt = k.astype(BF16).reshape(n_b, nk, tk, D_ATTN).transpose(0, 1, 3, 2)
        kit = ki.astype(BF16).reshape(n_b, nk, tk, IDX_DIM).transpose(0, 1, 3, 2)
        kit = jnp.concatenate([kit, kit], axis=2)
        at = _attn_prompt(qp, qip, wi, kt, vb.reshape(n_b, nk, tk, 2 * D_ATTN), kit, n_batch=n_b, t=t_p, tq=tq, tk=tk)
        xp1, xp1b = _merge(xp, ya, yb, at, gl, wo, wout, row1(ln1_g[l]), row1(ln1_b[l]), tm=tm_p)

        us, zcs, qs, ks, vs, _, qis = _in_proj_a(xsb, w_a, pad_heads=False, tm=tm_s)
        gls, kis, wis = _in_proj_b(xsb, w_b, tm=tm_s)
        seq3 = lambda a: a.reshape(n_s, t_s, a.shape[-1])
        tmajor = lambda a: seq3(a).transpose(1, 0, 2)
        nmajor = lambda a: a.transpose(1, 0, 2).reshape(m_s, a.shape[-1])
        yas, hsr, hsi = _s5_sample(tmajor(us), state_ssm_re[l].reshape(n_s, N_STATE),
                                   state_ssm_im[l].reshape(n_s, N_STATE), s5w, n=n_s, t=t_s)
        ybs, nbufs = _conv_sample(tmajor(zcs), state_conv[l].transpose(1, 0, 2), cw, n=n_s, t=t_s)
        yas, ybs = nmajor(yas), nmajor(ybs)
        ats = _attn_sample(page_table, seq3(qs), seq3(qis), seq3(wis), seq3(ks), seq3(vs), seq3(kis),
                           pool_ki, pool_k, pool_v, l, n=n_s, t=t_s)
        xs1, xs1b = _merge(xs, yas, ybs, ats.reshape(m_s, D_ATTN), gls, wo, wout, row1(ln1_g[l]), row1(ln1_b[l]),
                           tm=tm_s)

        j = l // 2
        g2, b2 = row1(ln2_g[l]), row1(ln2_b[l])
        if l % 2 == 0:
            wg, wu, wd = ffn_w_gate[j].astype(BF16), ffn_w_up[j].astype(BF16), ffn_w_down[j].astype(BF16)
            xp, xpb = _ffn(xp1, xp1b, wg, wu, wd, g2, b2, tm=tm_p, tf=tf)
            xs, xsb = _ffn(xs1, xs1b, wg, wu, wd, g2, b2, tm=tm_s, tf=tf)
        else:
            wg, wu, wd = moe_w_gate[j].astype(BF16), moe_w_up[j].astype(BF16), moe_w_down[j].astype(BF16)
            xp, xpb = _moe(xp1, xp1b, moe_w_router[j], wg, wu, wd, g2, b2, tm=tm_p, tf=tf)
            xs, xsb = _moe(xs1, xs1b, moe_w_router[j], wg, wu, wd, g2, b2, tm=tm_s, tf=tf)

        nb = CONV_WIDTH - 1
        layer_out = (
            k.reshape(n_b, t_p, N_HEADS, HEAD_DIM), v.reshape(n_b, t_p, N_HEADS, HEAD_DIM),
            ki.reshape(n_b, t_p, IDX_DIM),
            hfin[:, 0].reshape(n_b, N_SSM_GROUPS, SSM_STATE), hfin[:, 1].reshape(n_b, N_SSM_GROUPS, SSM_STATE),
            nbuf[:, CONV_HALO - nb:, :],
            ks.reshape(n_s, t_s, N_HEADS, HEAD_DIM), vs.reshape(n_s, t_s, N_HEADS, HEAD_DIM),
            kis.reshape(n_s, t_s, IDX_DIM),
            hsr.reshape(n_s, N_SSM_GROUPS, SSM_STATE), hsi.reshape(n_s, N_SSM_GROUPS, SSM_STATE),
            nbufs.transpose(1, 0, 2))
        for dst, val in zip(outs, layer_out):
            dst.append(val)

    return (xp.reshape(n_b, t_p, D_MODEL), xs.reshape(n_s, t_s, D_MODEL), *[jnp.stack(o) for o in outs])
```
